```python
import math
import jax, jax.numpy as jnp
from jax import lax
import numpy as np

D_MODEL = 1024
BATCH = 8
SEQ = 2048
DEPTH = 1
DEC_BATCH = 128
DEC_SEQ = 1
PAST_LEN = 16384
PAGE_SIZE = 128

D_INNER_M = 2 * D_MODEL
P_M = 64
H_M = D_INNER_M // P_M
G_M = 4
HG_M = H_M // G_M
N_M = 128
CONV_M = D_INNER_M + 2 * G_M * N_M
H_G = 8
DK = 128
DV = 128
CONV_G = 2 * H_G * DK + H_G * DV
CONV_K = 4
CHUNK = 64
D_FF = ((8 * D_MODEL // 3 + 255) // 256) * 256
PLE_DIM = 256
DT_MIN = 0.001
DT_MAX = 0.1
EPS = 1e-6
IN_SIZES = (D_INNER_M, CONV_M, H_M, CONV_G, H_G * DV, H_G, H_G, 2 * D_MODEL)
N_IN = sum(IN_SIZES)

kernel_name = "hybrid_ssd_gdn_gated_merge_step"


def _rms_norm(x, g):
    xf = x.astype(jnp.float32)
    y = xf * lax.rsqrt(jnp.mean(xf * xf, axis=-1, keepdims=True) + EPS)
    return (y * g.astype(jnp.float32)).astype(x.dtype)


def _l2norm(t):
    return t * lax.rsqrt(jnp.sum(t * t, axis=-1, keepdims=True) + EPS)


def _split(t, sizes):
    outs, start = [], 0
    for s in sizes:
        outs.append(t[..., start:start + s])
        start += s
    return outs


def _causal_conv(u, buf, w, b):
    seq = u.shape[1]
    full = jnp.concatenate([buf, u], axis=1)
    out = full[:, 0:seq] * w[0]
    for j in range(1, CONV_K):
        out = out + full[:, j:j + seq] * w[j]
    if b is not None:
        out = out + b
    return out, full[:, seq:]


def _chunking(seq):
    c = min(CHUNK, seq)
    n = -(-seq // c)
    return c, n, n * c - seq


def _pad_seq(t, pad):
    return jnp.pad(t, [(0, 0), (0, pad)] + [(0, 0)] * (t.ndim - 2))


def _ssd_chunked(x, dt, a, bm, cm, s0):
    bsz, seq = x.shape[:2]
    c, n, pad = _chunking(seq)
    x, dt, bm, cm = (_pad_seq(t, pad) for t in (x, dt, bm, cm))
    xdt = (x * dt[..., None]).reshape(bsz, n, c, G_M, HG_M, P_M)
    da = (dt * a).reshape(bsz, n, c, G_M, HG_M)
    bm = bm.reshape(bsz, n, c, G_M, N_M)
    cm = cm.reshape(bsz, n, c, G_M, N_M)
    acum = jnp.cumsum(da, axis=2)
    acum_t = jnp.moveaxis(acum, 2, -1)
    tril = jnp.tril(jnp.ones((c, c), bool))
    lmat = jnp.exp(jnp.where(tril, acum_t[..., :, None] - acum_t[..., None, :], -jnp.inf))
    cb = jnp.einsum('bnlgd,bnsgd->bngls', cm, bm)
    y_diag = jnp.einsum('bngls,bnghls,bnsghp->bnlghp', cb, lmat, xdt)
    decay_out = jnp.exp(acum[:, :, -1:] - acum)
    chunk_states = jnp.einsum('bnlgd,bnlgh,bnlghp->bnghpd', bm, decay_out, xdt)
    chunk_decay = jnp.exp(acum[:, :, -1])

    def step(s, inp):
        st, dec = inp
        return s * dec[..., None, None] + st, s

    s0 = s0.reshape(bsz, G_M, HG_M, P_M, N_M)
    s_final, prev = lax.scan(step, s0, (jnp.moveaxis(chunk_states, 1, 0), jnp.moveaxis(chunk_decay, 1, 0)))
    prev = jnp.moveaxis(prev, 0, 1)
    y_off = jnp.einsum('bnlgd,bnghpd,bnlgh->bnlghp', cm, prev, jnp.exp(acum))
    y = (y_diag + y_off).reshape(bsz, n * c, H_M, P_M)[:, :seq]
    return y, s_final.reshape(bsz, H_M, P_M, N_M)


def _gated_delta_chunked(q, k, v, g, beta, s0):
    bsz, seq = q.shape[:2]
    c, n, pad = _chunking(seq)
    q, k, v, g, beta = (_pad_seq(t, pad) for t in (q, k, v, g, beta))
    to_chunks = lambda t: jnp.moveaxis(t.reshape((bsz, n, c) + t.shape[2:]), 3, 1)
    q, k, v, g, beta = (to_chunks(t) for t in (q, k, v, g, beta))
    gcum = jnp.cumsum(g, axis=-1)
    tril = jnp.tril(jnp.ones((c, c), bool))
    strict = jnp.tril(jnp.ones((c, c), bool), -1)
    dmask = jnp.exp(jnp.where(tril, gcum[..., :, None] - gcum[..., None, :], -jnp.inf))
    kb = k * beta[..., None]
    a_mat = jnp.where(strict, jnp.einsum('bhnld,bhnsd->bhnls', kb, k) * dmask, 0.0)
    rhs = jnp.concatenate([v * beta[..., None], kb * jnp.exp(gcum)[..., None]], axis=-1)
    eye = jnp.eye(c, dtype=a_mat.dtype)
    sol = lax.linalg.triangular_solve(a_mat + eye, rhs, left_side=True, lower=True, unit_diagonal=True)
    u, w = sol[..., :DV], sol[..., DV:]
    qk = jnp.where(tril, jnp.einsum('bhnld,bhnsd->bhnls', q, k) * dmask, 0.0)
    qg = q * jnp.exp(gcum)[..., None]
    kg = k * jnp.exp(gcum[..., -1:] - gcum)[..., None]
    glast = jnp.exp(gcum[..., -1])

    def step(s, inp):
        u_c, w_c, qg_c, kg_c, qk_c, gl_c = inp
        v_new = u_c - jnp.einsum('bhld,bhde->bhle', w_c, s)
        o = jnp.einsum('bhld,bhde->bhle', qg_c, s) + jnp.einsum('bhls,bhse->bhle', qk_c, v_new)
        s = s * gl_c[..., None, None] + jnp.einsum('bhld,bhle->bhde', kg_c, v_new)
        return s, o

    xs = tuple(jnp.moveaxis(t, 2, 0) for t in (u, w, qg, kg, qk, glast))
    s_final, o = lax.scan(step, s0, xs)
    o = jnp.transpose(o, (1, 0, 3, 2, 4)).reshape(bsz, n * c, H_G, DV)[:, :seq]
    return o, s_final


def _layer(x, p, s_ssm, s_ssm_conv, s_gdn, s_gdn_conv,
           norm_mix, w_in, ssm_conv_w, ssm_conv_b, ssm_dt_bias, ssm_a_log, ssm_d, ssm_norm,
           gdn_conv_w, gdn_dt_bias, gdn_a_log, gdn_norm, w_branch_ssm, w_branch_gdn, w_out,
           norm_ffn, w_ffn_in, w_ffn_out, norm_pl, w_pl_gate, w_pl_proj):
    bsz, seq, _ = x.shape
    f32 = jnp.float32
    dtype = x.dtype
    h = _rms_norm(x, norm_mix)
    proj = jnp.einsum('bld,de->ble', h, w_in).astype(f32)
    z, xbc, dt_raw, qkv, gate, b_raw, a_raw, mgate = _split(proj, IN_SIZES)

    xbc, new_ssm_conv = _causal_conv(xbc, s_ssm_conv.astype(f32), ssm_conv_w.astype(f32), ssm_conv_b.astype(f32))
    xbc = jax.nn.silu(xbc)
    xs_m, bm, cm = _split(xbc, (D_INNER_M, G_M * N_M, G_M * N_M))
    xs_m = xs_m.reshape(bsz, seq, H_M, P_M)
    bm = bm.reshape(bsz, seq, G_M, N_M)
    cm = cm.reshape(bsz, seq, G_M, N_M)
    dt = jax.nn.softplus(dt_raw + ssm_dt_bias.astype(f32))
    a = -jnp.exp(ssm_a_log.astype(f32))
    y, new_ssm = _ssd_chunked(xs_m, dt, a, bm, cm, s_ssm.astype(f32))
    y = (y + xs_m * ssm_d.astype(f32)[:, None]).reshape(bsz, seq, D_INNER_M) * jax.nn.silu(z)
    y = _rms_norm(y.reshape(bsz, seq, G_M, D_INNER_M // G_M), ssm_norm.reshape(G_M, D_INNER_M // G_M))
    y = y.reshape(bsz, seq, D_INNER_M)

    qkv, new_gdn_conv = _causal_conv(qkv, s_gdn_conv.astype(f32), gdn_conv_w.astype(f32), None)
    qkv = jax.nn.silu(qkv)
    q, k, v = _split(qkv, (H_G * DK, H_G * DK, H_G * DV))
    q = _l2norm(q.reshape(bsz, seq, H_G, DK)) * (DK ** -0.5)
    k = _l2norm(k.reshape(bsz, seq, H_G, DK))
    v = v.reshape(bsz, seq, H_G, DV)
    beta = jax.nn.sigmoid(b_raw)
    g = -jnp.exp(gdn_a_log.astype(f32)) * jax.nn.softplus(a_raw + gdn_dt_bias.astype(f32))
    o, new_gdn = _gated_delta_chunked(q, k, v, g, beta, s_gdn.astype(f32))
    o = _rms_norm(o, gdn_norm) * jax.nn.silu(gate.reshape(bsz, seq, H_G, DV))
    o = o.reshape(bsz, seq, H_G * DV)

    g_ssm, g_gdn = jnp.split(jax.nn.sigmoid(mgate).astype(dtype), 2, axis=-1)
    mix = (g_ssm * jnp.einsum('ble,ed->bld', y.astype(dtype), w_branch_ssm)
           + g_gdn * jnp.einsum('ble,ed->bld', o.astype(dtype), w_branch_gdn))
    x = x + jnp.einsum('bld,de->ble', mix, w_out)

    gu = jnp.einsum('bld,df->blf', _rms_norm(x, norm_ffn), w_ffn_in)
    gt, up = jnp.split(gu, 2, axis=-1)
    x = x + jnp.einsum('blf,fd->bld', jax.nn.silu(gt) * up, w_ffn_out)

    pe = jnp.einsum('blp,pd->bld', p, w_pl_proj)
    x = x + pe * jax.nn.sigmoid(jnp.einsum('bld,de->ble', _rms_norm(x, norm_pl), w_pl_gate))
    return x, (new_ssm.astype(dtype), new_ssm_conv.astype(dtype), new_gdn.astype(dtype), new_gdn_conv.astype(dtype))


def setup_inputs(seed: int = 0) -> dict:
    key = jax.random.key(seed)
    ks = iter(jax.random.split(key, 48))
    f32 = jnp.float32

    def nrm(shape, scale):
        return jax.random.normal(next(ks), shape, f32) * scale

    def gain(shape):
        return 1.0 + nrm(shape, 0.02)

    def dt_bias(shape):
        u = jax.random.uniform(next(ks), shape, f32)
        dt = jnp.exp(u * (math.log(DT_MAX) - math.log(DT_MIN)) + math.log(DT_MIN))
        return dt + jnp.log(-jnp.expm1(-dt))

    def a_log(shape):
        return jnp.log(jax.random.uniform(next(ks), shape, f32, 1.0, 16.0))

    L = DEPTH
    return {
        "x_prompt": nrm((BATCH, SEQ, D_MODEL), 1.0),
        "x_sample": nrm((DEC_BATCH, DEC_SEQ, D_MODEL), 1.0),
        "p_prompt": nrm((DEPTH, BATCH, SEQ, PLE_DIM), 1.0),
        "p_sample": nrm((DEPTH, DEC_BATCH, DEC_SEQ, PLE_DIM), 1.0),
        "state_ssm": nrm((DEPTH, DEC_BATCH, H_M, P_M, N_M), 0.1),
        "state_ssm_conv": nrm((DEPTH, DEC_BATCH, CONV_K - 1, CONV_M), 1.0),
        "state_gdn": nrm((DEPTH, DEC_BATCH, H_G, DK, DV), 0.1),
        "state_gdn_conv": nrm((DEPTH, DEC_BATCH, CONV_K - 1, CONV_G), 1.0),
        "norm_mix": gain((L, D_MODEL)),
        "w_in": nrm((L, D_MODEL, N_IN), D_MODEL ** -0.5),
        "ssm_conv_w": nrm((L, CONV_K, CONV_M), CONV_K ** -0.5),
        "ssm_conv_b": nrm((L, CONV_M), 0.01),
        "ssm_dt_bias": dt_bias((L, H_M)),
        "ssm_a_log": a_log((L, H_M)),
        "ssm_d": 1.0 + nrm((L, H_M), 0.1),
        "ssm_norm": gain((L, D_INNER_M)),
        "gdn_conv_w": nrm((L, CONV_K, CONV_G), CONV_K ** -0.5),
        "gdn_dt_bias": dt_bias((L, H_G)),
        "gdn_a_log": a_log((L, H_G)),
        "gdn_norm": gain((L, DV)),
        "w_branch_ssm": nrm((L, D_INNER_M, D_MODEL), D_INNER_M ** -0.5),
        "w_branch_gdn": nrm((L, H_G * DV, D_MODEL), (H_G * DV) ** -0.5),
        "w_out": nrm((L, D_MODEL, D_MODEL), D_MODEL ** -0.5),
        "norm_ffn": gain((L, D_MODEL)),
        "w_ffn_in": nrm((L, D_MODEL, 2 * D_FF), D_MODEL ** -0.5),
        "w_ffn_out": nrm((L, D_FF, D_MODEL), D_FF ** -0.5),
        "norm_pl": gain((L, D_MODEL)),
        "w_pl_gate": nrm((L, D_MODEL, D_MODEL), D_MODEL ** -0.5),
        "w_pl_proj": nrm((L, PLE_DIM, D_MODEL), PLE_DIM ** -0.5),
        "norm_final": gain((D_MODEL,)),
    }


def reference(x_prompt, x_sample, p_prompt, p_sample, state_ssm, state_ssm_conv, state_gdn, state_gdn_conv,
              norm_mix, w_in, ssm_conv_w, ssm_conv_b, ssm_dt_bias, ssm_a_log, ssm_d, ssm_norm,
              gdn_conv_w, gdn_dt_bias, gdn_a_log, gdn_norm, w_branch_ssm, w_branch_gdn, w_out,
              norm_ffn, w_ffn_in, w_ffn_out, norm_pl, w_pl_gate, w_pl_proj, norm_final):
    def run(x, p, st_ssm, st_ssm_conv, st_gdn, st_gdn_conv):
        new = ([], [], [], [])
        for i in range(DEPTH):
            x, states = _layer(x, p[i], st_ssm[i], st_ssm_conv[i], st_gdn[i], st_gdn_conv[i],
                               norm_mix[i], w_in[i], ssm_conv_w[i], ssm_conv_b[i], ssm_dt_bias[i],
                               ssm_a_log[i], ssm_d[i], ssm_norm[i], gdn_conv_w[i], gdn_dt_bias[i],
                               gdn_a_log[i], gdn_norm[i], w_branch_ssm[i], w_branch_gdn[i], w_out[i],
                               norm_ffn[i], w_ffn_in[i], w_ffn_out[i], norm_pl[i], w_pl_gate[i], w_pl_proj[i])
            for lst, s in zip(new, states):
                lst.append(s)
        return _rms_norm(x, norm_final), [jnp.stack(lst) for lst in new]

    bp = x_prompt.shape[0]
    dt = x_prompt.dtype
    z_ssm = jnp.zeros((DEPTH, bp, H_M, P_M, N_M), dt)
    z_ssm_conv = jnp.zeros((DEPTH, bp, CONV_K - 1, CONV_M), dt)
    z_gdn = jnp.zeros((DEPTH, bp, H_G, DK, DV), dt)
    z_gdn_conv = jnp.zeros((DEPTH, bp, CONV_K - 1, CONV_G), dt)
    y_prompt, ps = run(x_prompt, p_prompt, z_ssm, z_ssm_conv, z_gdn, z_gdn_conv)
    y_sample, ss = run(x_sample, p_sample, state_ssm, state_ssm_conv, state_gdn, state_gdn_conv)
    return (y_prompt, y_sample, ps[0], ps[1], ps[2], ps[3], ss[0], ss[1], ss[2], ss[3])
```

```python
import functools

import jax
import jax.numpy as jnp
from jax import lax
from jax.experimental import pallas as pl
from jax.experimental.pallas import tpu as pltpu

f32 = jnp.float32
bf16 = jnp.bfloat16
HIGHEST = lax.Precision.HIGHEST

EPS = 1e-6
CONV_K = 4
LANES = 128
SUBLANES = 8
VMEM_LIMIT = 56 * 1024 * 1024

P_M = 64
N_M = 128
G_M = 4
DK = 128
DV = 128
SSD_BLOCK = 128
GDN_CHUNK = 64
GDN_BLOCK = 128
SAMPLE_BB = 8


def _nt(a, b):
    return lax.dot_general(a, b, (((1,), (1,)), ((), ())), preferred_element_type=f32)


def _tn(a, b, precision=None):
    return lax.dot_general(a, b, (((0,), (0,)), ((), ())), preferred_element_type=f32, precision=precision)


def _mm(a, b):
    return jnp.dot(a, b, preferred_element_type=f32)


def _mm_hi(a, b):
    return jnp.dot(a, b, preferred_element_type=f32, precision=HIGHEST)


def _silu(x):
    return x * jax.nn.sigmoid(x)


def _softplus(x):
    return jnp.maximum(x, 0.0) + jnp.log1p(jnp.exp(-jnp.abs(x)))


def _rms(x, g):
    return x * lax.rsqrt(jnp.mean(x * x, axis=-1, keepdims=True) + EPS) * g


def _iota2(shape):
    return lax.broadcasted_iota(jnp.int32, shape, 0), lax.broadcasted_iota(jnp.int32, shape, 1)


def _inproj_kernel(x_ref, g_ref, w_ref, o_ref, h_ref):
    @pl.when(pl.program_id(1) == 0)
    def _():
        h_ref[...] = _rms(x_ref[...], g_ref[...]).astype(bf16)

    o_ref[...] = _mm(h_ref[...], w_ref[...])


def _inproj(x, g, w, tm, tn):
    t, d = x.shape
    npad = w.shape[1]
    return pl.pallas_call(
        _inproj_kernel,
        grid=(t // tm, npad // tn),
        in_specs=[
            pl.BlockSpec((tm, d), lambda i, j: (i, 0)),
            pl.BlockSpec((1, d), lambda i, j: (0, 0)),
            pl.BlockSpec((d, tn), lambda i, j: (0, j)),
        ],
        out_specs=pl.BlockSpec((tm, tn), lambda i, j: (i, j)),
        out_shape=jax.ShapeDtypeStruct((t, npad), f32),
        scratch_shapes=[pltpu.VMEM((tm, d), bf16)],
        compiler_params=pltpu.CompilerParams(
            dimension_semantics=("parallel", "arbitrary"), vmem_limit_bytes=VMEM_LIMIT),
        name="inproj",
    )(x, g, w)


def _conv_block(u_ref, ext_ref, act_ref, cw_ref, cb_ref, rows, width, chunk=512):
    for c0 in range(0, width, chunk):
        sl = slice(c0, c0 + chunk)
        u = u_ref[:, sl]
        ext_ref[SUBLANES:SUBLANES + rows, sl] = u
        cw = cw_ref[:, sl]
        conv = u * cw[3:4]
        for j in range(CONV_K - 1):
            off = SUBLANES - (CONV_K - 1) + j
            conv = conv + ext_ref[off:off + rows, sl] * cw[j:j + 1]
        if cb_ref is not None:
            conv = conv + cb_ref[:, sl]
        act_ref[:, sl] = _silu(conv)
        ext_ref[0:SUBLANES, sl] = ext_ref[rows:rows + SUBLANES, sl]


def _ssd_kernel(xbc_ref, z_ref, dt_ref, cw_ref, cb_ref, dtb_ref, aneg_ref, dexp_ref, nw_ref, e_ref,
                y_ref, st_out_ref, ext_ref, act_ref, st_ref, yd_ref):
    i = pl.program_id(1)
    c = SSD_BLOCK
    d_inner = y_ref.shape[1]
    gw = d_inner // G_M
    hg = gw // P_M

    @pl.when(i == 0)
    def _():
        ext_ref[0:SUBLANES, :] = jnp.zeros((SUBLANES, ext_ref.shape[1]), f32)
        st_ref[...] = jnp.zeros(st_ref.shape, f32)

    _conv_block(xbc_ref, ext_ref, act_ref, cw_ref, cb_ref, c, act_ref.shape[1])

    r, cc = _iota2((c, c))
    lower = r >= cc
    dt = _softplus(dt_ref[...] + dtb_ref[...])
    da = dt * aneg_ref[...]
    acum = _mm_hi(jnp.where(lower, 1.0, 0.0), da)
    acum_t = _mm_hi(da.T, jnp.where(r <= cc, 1.0, 0.0))
    alast = acum[c - 1:c, :]
    dout = jnp.exp(alast - acum)
    scal = jnp.concatenate(
        [dt, dt * dout, jnp.exp(acum), jnp.broadcast_to(jnp.exp(alast), (SUBLANES, LANES))], axis=0)

    for g in range(G_M):
        gs = slice(g * gw, (g + 1) * gw)
        se = _mm_hi(scal, e_ref[:, gs])
        dt_e, dd_e, ea_e, cd_e = se[0:c], se[c:2 * c], se[2 * c:3 * c], se[3 * c:3 * c + 1]
        xs = act_ref[:, gs]
        bm = act_ref[:, d_inner + g * N_M:d_inner + (g + 1) * N_M].astype(bf16)
        cm = act_ref[:, d_inner + (G_M + g) * N_M:d_inner + (G_M + g + 1) * N_M].astype(bf16)
        xdt = (xs * dt_e).astype(bf16)
        cb = _nt(cm, bm)
        st = st_ref[:, gs]
        y_off = _mm(cm, st.astype(bf16)) * ea_e
        st_ref[:, gs] = st * cd_e + _tn(bm, (xs * dd_e).astype(bf16))
        for hh in range(hg):
            h = g * hg + hh
            lmat = jnp.exp(jnp.where(lower, acum[:, h:h + 1] - acum_t[h:h + 1, :], -jnp.inf))
            yd_ref[:, hh * P_M:(hh + 1) * P_M] = _mm((cb * lmat).astype(bf16), xdt[:, hh * P_M:(hh + 1) * P_M])
        y = yd_ref[...] + y_off + xs * dexp_ref[:, gs]
        y = y * _silu(z_ref[:, gs])
        y_ref[:, gs] = _rms(y, nw_ref[:, gs]).astype(y_ref.dtype)

    @pl.when(i == pl.num_programs(1) - 1)
    def _():
        for j in range(d_inner // LANES):
            st_out_ref[0, j * LANES:(j + 1) * LANES, :] = st_ref[:, j * LANES:(j + 1) * LANES].T


def _ssd_prompt(proj, bsz, seq, off, cw, cb, dtb, aneg, dexp, nw, e_mat):
    d_inner = dexp.shape[1]
    conv_w = cw.shape[1]
    nb = seq // SSD_BLOCK
    c = SSD_BLOCK
    row = lambda b, i: b * nb + i
    const = lambda shape: pl.BlockSpec(shape, lambda b, i: (0, 0))
    return pl.pallas_call(
        _ssd_kernel,
        grid=(bsz, nb),
        in_specs=[
            pl.BlockSpec((c, conv_w), lambda b, i: (row(b, i), off["xbc"] // conv_w)),
            pl.BlockSpec((c, d_inner), lambda b, i: (row(b, i), off["z"] // d_inner)),
            pl.BlockSpec((c, LANES), lambda b, i: (row(b, i), off["dt"] // LANES)),
            const(cw.shape), const(cb.shape), const(dtb.shape), const(aneg.shape), const(dexp.shape),
            const(nw.shape), const(e_mat.shape),
        ],
        out_specs=[
            pl.BlockSpec((c, d_inner), lambda b, i: (row(b, i), 0)),
            pl.BlockSpec((1, d_inner, N_M), lambda b, i: (b, 0, 0)),
        ],
        out_shape=[
            jax.ShapeDtypeStruct((bsz * seq, d_inner), bf16),
            jax.ShapeDtypeStruct((bsz, d_inner, N_M), f32),
        ],
        scratch_shapes=[
            pltpu.VMEM((c + SUBLANES, conv_w), f32),
            pltpu.VMEM((c, conv_w), f32),
            pltpu.VMEM((N_M, d_inner), f32),
            pltpu.VMEM((c, d_inner // G_M), f32),
        ],
        compiler_params=pltpu.CompilerParams(
            dimension_semantics=("parallel", "arbitrary"), vmem_limit_bytes=VMEM_LIMIT),
        name="ssd_prompt",
    )(proj, proj, proj, cw, cb, dtb, aneg, dexp, nw, e_mat)


def _unit_lower_inverse(a_strict, eye):
    n = a_strict.shape[0]
    q = -a_strict
    p = eye + q
    span = 2
    while span < n:
        q = _mm_hi(q, q)
        p = p + _mm_hi(p, q)
        span *= 2
    return p


def _gdn_kernel(qkv_ref, gate_ref, b_ref, a_ref, cw_ref, dtb_ref, aneg_ref, nw_ref,
                o_ref, st_out_ref, ext_ref, act_ref, st_ref):
    i = pl.program_id(1)
    blk = GDN_BLOCK
    ck = GDN_CHUNK
    n_heads = st_ref.shape[0]
    hk = n_heads * DK

    @pl.when(i == 0)
    def _():
        ext_ref[0:SUBLANES, :] = jnp.zeros((SUBLANES, ext_ref.shape[1]), f32)
        st_ref[...] = jnp.zeros(st_ref.shape, f32)

    _conv_block(qkv_ref, ext_ref, act_ref, cw_ref, None, blk, act_ref.shape[1])

    r, cc = _iota2((blk, blk))
    same = (r // ck) == (cc // ck)
    beta = jax.nn.sigmoid(b_ref[...])
    glog = aneg_ref[...] * _softplus(a_ref[...] + dtb_ref[...])
    gcum = _mm_hi(jnp.where(same & (r >= cc), 1.0, 0.0), glog)
    gcum_t = _mm_hi(glog.T, jnp.where(same & (r <= cc), 1.0, 0.0))

    rk, ck_i = _iota2((ck, ck))
    lower = rk >= ck_i
    strict = rk > ck_i
    eye = jnp.where(rk == ck_i, 1.0, 0.0)

    for h in range(n_heads):
        q = act_ref[:, h * DK:(h + 1) * DK]
        k = act_ref[:, hk + h * DK:hk + (h + 1) * DK]
        v = act_ref[:, 2 * hk + h * DV:2 * hk + (h + 1) * DV]
        qn = q * lax.rsqrt(jnp.sum(q * q, axis=-1, keepdims=True) + EPS) * (DK ** -0.5)
        kn = k * lax.rsqrt(jnp.sum(k * k, axis=-1, keepdims=True) + EPS)
        bcol = beta[:, h:h + 1]
        gcol = gcum[:, h:h + 1]
        eg = jnp.exp(gcol)
        s = st_ref[h]
        for j in range(blk // ck):
            rs = slice(j * ck, (j + 1) * ck)
            dm = jnp.exp(jnp.where(lower, gcol[rs] - gcum_t[h:h + 1, rs], -jnp.inf))
            knb = kn[rs].astype(bf16)
            kb = kn[rs] * bcol[rs]
            a_mat = jnp.where(strict, _nt(kb.astype(bf16), knb) * dm, 0.0)
            tinv = _unit_lower_inverse(a_mat, eye)
            rhs = jnp.concatenate([v[rs] * bcol[rs], kb * eg[rs]], axis=-1)
            sol = _mm(tinv.astype(bf16), rhs.astype(bf16))
            u, w = sol[:, :DV], sol[:, DV:]
            glast = gcol[(j + 1) * ck - 1:(j + 1) * ck]
            qkm = jnp.where(lower, _nt(qn[rs].astype(bf16), knb) * dm, 0.0)
            sb = s.astype(bf16)
            vn = u - _mm(w.astype(bf16), sb)
            o = _mm((qn[rs] * eg[rs]).astype(bf16), sb) + _mm(qkm.astype(bf16), vn.astype(bf16))
            s = s * jnp.exp(glast) + _tn((kn[rs] * jnp.exp(glast - gcol[rs])).astype(bf16), vn.astype(bf16))
            gt = gate_ref[rs, h * DV:(h + 1) * DV]
            o_ref[rs, h * DV:(h + 1) * DV] = (_rms(o, nw_ref[...]) * _silu(gt)).astype(o_ref.dtype)
        st_ref[h] = s

    @pl.when(i == pl.num_programs(1) - 1)
    def _():
        st_out_ref[0] = st_ref[...]


def _gdn_prompt(proj, bsz, seq, off, cw, dtb, aneg, nw, n_heads):
    conv_w = cw.shape[1]
    hv = n_heads * DV
    blk = GDN_BLOCK
    nb = seq // blk
    row = lambda b, i: b * nb + i
    const = lambda shape: pl.BlockSpec(shape, lambda b, i: (0, 0))
    return pl.pallas_call(
        _gdn_kernel,
        grid=(bsz, nb),
        in_specs=[
            pl.BlockSpec((blk, conv_w), lambda b, i: (row(b, i), off["qkv"] // conv_w)),
            pl.BlockSpec((blk, hv), lambda b, i: (row(b, i), off["gate"] // hv)),
            pl.BlockSpec((blk, LANES), lambda b, i: (row(b, i), off["b"] // LANES)),
            pl.BlockSpec((blk, LANES), lambda b, i: (row(b, i), off["a"] // LANES)),
            const(cw.shape), const(dtb.shape), const(aneg.shape), const(nw.shape),
        ],
        out_specs=[
            pl.BlockSpec((blk, hv), lambda b, i: (row(b, i), 0)),
            pl.BlockSpec((1, n_heads, DK, DV), lambda b, i: (b, 0, 0, 0)),
        ],
        out_shape=[
            jax.ShapeDtypeStruct((bsz * seq, hv), bf16),
            jax.ShapeDtypeStruct((bsz, n_heads, DK, DV), f32),
        ],
        scratch_shapes=[
            pltpu.VMEM((blk + SUBLANES, conv_w), f32),
            pltpu.VMEM((blk, conv_w), f32),
            pltpu.VMEM((n_heads, DK, DV), f32),
        ],
        compiler_params=pltpu.CompilerParams(
            dimension_semantics=("parallel", "arbitrary"), vmem_limit_bytes=VMEM_LIMIT),
        name="gdn_prompt",
    )(proj, proj, proj, proj, cw, dtb, aneg, nw)


def _conv_step(u_ref, cs_ref, cs_out_ref, cw_ref, cb_ref):
    u = u_ref[...]
    cw = cw_ref[...]
    conv = u * cw[3:4]
    for j in range(CONV_K - 1):
        conv = conv + cs_ref[j] * cw[j:j + 1]
    if cb_ref is not None:
        conv = conv + cb_ref[...]
    for j in range(CONV_K - 2):
        cs_out_ref[j] = cs_ref[j + 1]
    cs_out_ref[CONV_K - 2] = u
    return _silu(conv)


def _split3(x):
    hi = x.astype(bf16)
    r1 = x - hi.astype(f32)
    mid = r1.astype(bf16)
    lo = (r1 - mid.astype(f32)).astype(bf16)
    return hi, mid, lo


def _ssd_step_kernel(xbc_ref, z_ref, dt_ref, cs_ref, st_ref, cw_ref, cb_ref, dtb_ref, aneg_ref, dexp_ref,
                     nw_ref, e_ref, y_ref, cs_out_ref, st_out_ref, yacc_ref):
    bb = xbc_ref.shape[0]
    d_inner = y_ref.shape[1]
    gw = d_inner // G_M
    act = _conv_step(xbc_ref, cs_ref, cs_out_ref, cw_ref, cb_ref)
    dt = _softplus(dt_ref[...] + dtb_ref[...])
    dec = jnp.exp(dt * aneg_ref[...])
    se = _mm_hi(jnp.concatenate([dt, dec], axis=0), e_ref[...])
    xs = act[:, :d_inner]
    xdt = xs * se[0:bb]
    dec_e = se[bb:2 * bb]
    d3 = jnp.concatenate(_split3(dec_e), axis=0)
    ri, _ = _iota2((bb, LANES))
    r3, _ = _iota2((3 * bb, LANES))

    def body(b, carry):
        pick = ri == b
        ones3 = jnp.where((r3 == b) | (r3 == b + bb) | (r3 == b + 2 * bb), 1.0, 0.0).astype(bf16)
        for g in range(G_M):
            gs = slice(g * gw, (g + 1) * gw)
            bm = act[:, d_inner + g * N_M:d_inner + (g + 1) * N_M]
            cm = act[:, d_inner + (G_M + g) * N_M:d_inner + (G_M + g + 1) * N_M]
            dcol = _tn(d3[:, gs], ones3)
            outer = _tn(xdt[:, gs].astype(bf16), jnp.where(pick, bm, 0.0).astype(bf16))
            s_new = st_ref[b, gs, :] * dcol + outer
            st_out_ref[b, gs, :] = s_new
            yacc_ref[:, gs] = _nt(cm.astype(bf16), s_new.astype(bf16))
        return jnp.where(pick[:, 0:1], yacc_ref[...], carry)

    yssm = lax.fori_loop(0, bb, body, jnp.zeros((bb, d_inner), f32))
    y = yssm + xs * dexp_ref[...]
    y = y * _silu(z_ref[...])
    for g in range(G_M):
        gs = slice(g * gw, (g + 1) * gw)
        y_ref[:, gs] = _rms(y[:, gs], nw_ref[:, gs]).astype(y_ref.dtype)


def _ssd_sample(proj, off, conv_state, state, cw, cb, dtb, aneg, dexp, nw, e_mat):
    bsz = proj.shape[0]
    d_inner = dexp.shape[1]
    conv_w = cw.shape[1]
    bb = SAMPLE_BB
    const = lambda shape: pl.BlockSpec(shape, lambda i: (0, 0))
    return pl.pallas_call(
        _ssd_step_kernel,
        grid=(bsz // bb,),
        in_specs=[
            pl.BlockSpec((bb, conv_w), lambda i: (i, off["xbc"] // conv_w)),
            pl.BlockSpec((bb, d_inner), lambda i: (i, off["z"] // d_inner)),
            pl.BlockSpec((bb, LANES), lambda i: (i, off["dt"] // LANES)),
            pl.BlockSpec((CONV_K - 1, bb, conv_w), lambda i: (0, i, 0)),
            pl.BlockSpec((bb, d_inner, N_M), lambda i: (i, 0, 0)),
            const(cw.shape), const(cb.shape), const(dtb.shape), const(aneg.shape), const(dexp.shape),
            const(nw.shape), const(e_mat.shape),
        ],
        out_specs=[
            pl.BlockSpec((bb, d_inner), lambda i: (i, 0)),
            pl.BlockSpec((CONV_K - 1, bb, conv_w), lambda i: (0, i, 0)),
            pl.BlockSpec((bb, d_inner, N_M), lambda i: (i, 0, 0)),
        ],
        out_shape=[
            jax.ShapeDtypeStruct((bsz, d_inner), bf16),
            jax.ShapeDtypeStruct((CONV_K - 1, bsz, conv_w), f32),
            jax.ShapeDtypeStruct((bsz, d_inner, N_M), f32),
        ],
        scratch_shapes=[pltpu.VMEM((bb, d_inner), f32)],
        compiler_params=pltpu.CompilerParams(
            dimension_semantics=("parallel",), vmem_limit_bytes=VMEM_LIMIT),
        name="ssd_sample",
    )(proj, proj, proj, conv_state, state, cw, cb, dtb, aneg, dexp, nw, e_mat)


def _gdn_step_kernel(qkv_ref, gate_ref, b_ref, a_ref, cs_ref, st_ref, cw_ref, dtb_ref, aneg_ref, nw_ref,
                     o_ref, cs_out_ref, st_out_ref):
    bb = qkv_ref.shape[0]
    n_heads = st_ref.shape[1]
    hk = n_heads * DK
    act = _conv_step(qkv_ref, cs_ref, cs_out_ref, cw_ref, None)
    beta = jax.nn.sigmoid(b_ref[...])
    eg = jnp.exp(aneg_ref[...] * _softplus(a_ref[...] + dtb_ref[...]))
    ri, _ = _iota2((bb, DV))

    heads = []
    for h in range(n_heads):
        q = act[:, h * DK:(h + 1) * DK]
        k = act[:, hk + h * DK:hk + (h + 1) * DK]
        v = act[:, 2 * hk + h * DV:2 * hk + (h + 1) * DV]
        qn = q * lax.rsqrt(jnp.sum(q * q, axis=-1, keepdims=True) + EPS) * (DK ** -0.5)
        kn = k * lax.rsqrt(jnp.sum(k * k, axis=-1, keepdims=True) + EPS)
        bcol = beta[:, h:h + 1]
        ecol = eg[:, h:h + 1]
        kb = kn * bcol
        heads.append(dict(
            u=v * bcol, w=(kb * ecol).astype(bf16), qg=(qn * ecol).astype(bf16),
            qk=jnp.sum(qn.astype(bf16).astype(f32) * kn.astype(bf16).astype(f32), axis=-1, keepdims=True),
            kn=kn.astype(bf16), ecol=ecol))

    def body(b, carry):
        pick = ri == b
        outs = []
        for h in range(n_heads):
            hd = heads[h]
            s = st_ref[b, h]
            sb = s.astype(bf16)
            vn = hd["u"] - _mm(hd["w"], sb)
            o = _mm(hd["qg"], sb) + hd["qk"] * vn
            vn_b = jnp.where(pick, vn, 0.0).astype(bf16)
            gl = jnp.sum(jnp.where(pick[:, 0:1], hd["ecol"], 0.0), axis=0, keepdims=True)
            st_out_ref[b, h] = s * gl + _tn(hd["kn"], vn_b)
            outs.append(jnp.where(pick, o, carry[h]))
        return tuple(outs)

    init = tuple(jnp.zeros((bb, DV), f32) for _ in range(n_heads))
    outs = lax.fori_loop(0, bb, body, init)
    for h in range(n_heads):
        gt = gate_ref[:, h * DV:(h + 1) * DV]
        o_ref[:, h * DV:(h + 1) * DV] = (_rms(outs[h], nw_ref[...]) * _silu(gt)).astype(o_ref.dtype)


def _gdn_sample(proj, off, conv_state, state, cw, dtb, aneg, nw):
    bsz = proj.shape[0]
    n_heads = state.shape[1]
    conv_w = cw.shape[1]
    hv = n_heads * DV
    bb = SAMPLE_BB
    const = lambda shape: pl.BlockSpec(shape, lambda i: (0, 0))
    return pl.pallas_call(
        _gdn_step_kernel,
        grid=(bsz // bb,),
        in_specs=[
            pl.BlockSpec((bb, conv_w), lambda i: (i, off["qkv"] // conv_w)),
            pl.BlockSpec((bb, hv), lambda i: (i, off["gate"] // hv)),
            pl.BlockSpec((bb, LANES), lambda i: (i, off["b"] // LANES)),
            pl.BlockSpec((bb, LANES), lambda i: (i, off["a"] // LANES)),
            pl.BlockSpec((CONV_K - 1, bb, conv_w), lambda i: (0, i, 0)),
            pl.BlockSpec((bb, n_heads, DK, DV), lambda i: (i, 0, 0, 0)),
            const(cw.shape), const(dtb.shape), const(aneg.shape), const(nw.shape),
        ],
        out_specs=[
            pl.BlockSpec((bb, hv), lambda i: (i, 0)),
            pl.BlockSpec((CONV_K - 1, bb, conv_w), lambda i: (0, i, 0)),
            pl.BlockSpec((bb, n_heads, DK, DV), lambda i: (i, 0, 0, 0)),
        ],
        out_shape=[
            jax.ShapeDtypeStruct((bsz, hv), bf16),
            jax.ShapeDtypeStruct((CONV_K - 1, bsz, conv_w), f32),
            jax.ShapeDtypeStruct((bsz, n_heads, DK, DV), f32),
        ],
        compiler_params=pltpu.CompilerParams(
            dimension_semantics=("parallel",), vmem_limit_bytes=VMEM_LIMIT),
        name="gdn_sample",
    )(proj, proj, proj, proj, conv_state, state, cw, dtb, aneg, nw)


def _tail_kernel(x_ref, y_ref, o_ref, mg_ref, p_ref, wbs_ref, wbg_ref, wout_ref, nf_ref, wfi_ref, wfo_ref,
                 npl_ref, wpg_ref, wpp_ref, nfin_ref, out_ref, *, ff_chunks, final_norm):
    d = x_ref.shape[1]
    d_ff = wfo_ref.shape[0]
    mg = mg_ref[...]
    mix = (jax.nn.sigmoid(mg[:, :d]) * _mm(y_ref[...], wbs_ref[...])
           + jax.nn.sigmoid(mg[:, d:]) * _mm(o_ref[...], wbg_ref[...]))
    x = x_ref[...] + _mm(mix.astype(bf16), wout_ref[...])
    h = _rms(x, nf_ref[...]).astype(bf16)
    fc = d_ff // ff_chunks
    for c in range(ff_chunks):
        gt = _mm(h, wfi_ref[:, c * fc:(c + 1) * fc])
        up = _mm(h, wfi_ref[:, d_ff + c * fc:d_ff + (c + 1) * fc])
        x = x + _mm((_silu(gt) * up).astype(bf16), wfo_ref[c * fc:(c + 1) * fc, :])
    pe = _mm(p_ref[...].astype(bf16), wpp_ref[...])
    x = x + pe * jax.nn.sigmoid(_mm(_rms(x, npl_ref[...]).astype(bf16), wpg_ref[...]))
    if final_norm:
        x = _rms(x, nfin_ref[...])
    out_ref[...] = x


def _tail(x, y, o, proj, off, p, wts, tm, final_norm):
    t, d = x.shape
    tok = lambda w: pl.BlockSpec((tm, w), lambda i: (i, 0))
    res = lambda a: pl.BlockSpec(a.shape, lambda i: (0, 0), pipeline_mode=pl.Buffered(1))
    names = ("wbs", "wbg", "wout", "nf", "wfi", "wfo", "npl", "wpg", "wpp", "nfin")
    return pl.pallas_call(
        functools.partial(_tail_kernel, ff_chunks=2, final_norm=final_norm),
        grid=(t // tm,),
        in_specs=[tok(d), tok(y.shape[1]), tok(o.shape[1]),
                  pl.BlockSpec((tm, 2 * d), lambda i: (i, off["mgate"] // (2 * d))),
                  tok(p.shape[1])] + [res(wts[n]) for n in names],
        out_specs=tok(d),
        out_shape=jax.ShapeDtypeStruct((t, d), f32),
        compiler_params=pltpu.CompilerParams(
            dimension_semantics=("parallel",), vmem_limit_bytes=VMEM_LIMIT),
        name="tail",
    )(x, y, o, proj, p, *[wts[n] for n in names])


def _pad_lanes(v, width=LANES):
    return jnp.pad(v, ((0, 0), (0, width - v.shape[1])))


def _layer_weights(i, d_model, norm_mix, w_in, ssm_conv_w, ssm_conv_b, ssm_dt_bias, ssm_a_log, ssm_d, ssm_norm,
                   gdn_conv_w, gdn_dt_bias, gdn_a_log, gdn_norm, w_branch_ssm, w_branch_gdn, w_out,
                   norm_ffn, w_ffn_in, w_ffn_out, norm_pl, w_pl_gate, w_pl_proj, norm_final):
    h_m = ssm_dt_bias.shape[1]
    h_g = gdn_dt_bias.shape[1]
    d_inner = h_m * P_M
    conv_m = ssm_conv_w.shape[2]
    conv_g = gdn_conv_w.shape[2]
    sizes = (d_inner, conv_m, h_m, conv_g, h_g * DV, h_g, h_g, 2 * d_model)
    starts = [0]
    for s in sizes:
        starts.append(starts[-1] + s)
    seg = lambda j: w_in[i][:, starts[j]:starts[j + 1]]
    z, xbc, dtw, qkv, gate, bw, aw, mg = (seg(j) for j in range(8))
    order = [("xbc", xbc), ("qkv", qkv), ("z", z), ("mgate", mg), ("gate", gate),
             ("dt", _pad_lanes(dtw)), ("b", _pad_lanes(bw)), ("a", _pad_lanes(aw))]
    off, pos = {}, 0
    for name, w in order:
        off[name] = pos
        pos += w.shape[1]
    w1 = jnp.concatenate([w for _, w in order], axis=1).astype(bf16)
    head_of = jnp.arange(d_inner, dtype=jnp.int32) // P_M
    e_mat = (jnp.arange(LANES, dtype=jnp.int32)[:, None] == head_of[None, :]).astype(f32)
    row = lambda v: v.reshape(1, -1).astype(f32)
    return dict(
        off=off, w1=w1, norm_mix=row(norm_mix[i]),
        ssm=dict(cw=ssm_conv_w[i], cb=row(ssm_conv_b[i]), dtb=_pad_lanes(row(ssm_dt_bias[i])),
                 aneg=_pad_lanes(-jnp.exp(row(ssm_a_log[i]))), dexp=row(jnp.repeat(ssm_d[i], P_M)),
                 nw=row(ssm_norm[i]), e_mat=e_mat),
        gdn=dict(cw=gdn_conv_w[i], dtb=_pad_lanes(row(gdn_dt_bias[i])),
                 aneg=_pad_lanes(-jnp.exp(row(gdn_a_log[i]))), nw=row(gdn_norm[i])),
        tail=dict(wbs=w_branch_ssm[i].astype(bf16), wbg=w_branch_gdn[i].astype(bf16), wout=w_out[i].astype(bf16),
                  nf=row(norm_ffn[i]), wfi=w_ffn_in[i].astype(bf16), wfo=w_ffn_out[i].astype(bf16),
                  npl=row(norm_pl[i]), wpg=w_pl_gate[i].astype(bf16), wpp=w_pl_proj[i].astype(bf16),
                  nfin=row(norm_final)),
        h_g=h_g,
    )


def _pick_tile(n, candidates):
    for c in candidates:
        if n % c == 0:
            return c
    raise ValueError(f"no tile for {n}")


def kernel(x_prompt, x_sample, p_prompt, p_sample, state_ssm, state_ssm_conv, state_gdn, state_gdn_conv,
           norm_mix, w_in, ssm_conv_w, ssm_conv_b, ssm_dt_bias, ssm_a_log, ssm_d, ssm_norm,
           gdn_conv_w, gdn_dt_bias, gdn_a_log, gdn_norm, w_branch_ssm, w_branch_gdn, w_out,
           norm_ffn, w_ffn_in, w_ffn_out, norm_pl, w_pl_gate, w_pl_proj, norm_final):
    depth = p_prompt.shape[0]
    bp, seq, d_model = x_prompt.shape
    bs, dec_seq, _ = x_sample.shape
    assert dec_seq == 1 and seq % SSD_BLOCK == 0 and seq % GDN_BLOCK == 0 and bs % SAMPLE_BB == 0
    h_m, h_g = ssm_dt_bias.shape[1], gdn_dt_bias.shape[1]
    d_inner = h_m * P_M
    conv_m, conv_g = ssm_conv_w.shape[2], gdn_conv_w.shape[2]

    xp = x_prompt.reshape(bp * seq, d_model)
    xs = x_sample.reshape(bs, d_model)
    new_p = ([], [], [], [])
    new_s = ([], [], [], [])
    for i in range(depth):
        lw = _layer_weights(i, d_model, norm_mix, w_in, ssm_conv_w, ssm_conv_b, ssm_dt_bias, ssm_a_log, ssm_d,
                            ssm_norm, gdn_conv_w, gdn_dt_bias, gdn_a_log, gdn_norm, w_branch_ssm, w_branch_gdn,
                            w_out, norm_ffn, w_ffn_in, w_ffn_out, norm_pl, w_pl_gate, w_pl_proj, norm_final)
        off = lw["off"]
        npad = lw["w1"].shape[1]
        tn = _pick_tile(npad, (1664, 896, 128))
        last = i == depth - 1

        tp = bp * seq
        proj = _inproj(xp, lw["norm_mix"], lw["w1"], _pick_tile(tp, (1024, 512, 256, 128)), tn)
        y, st_ssm = _ssd_prompt(proj, bp, seq, off, **lw["ssm"])
        o, st_gdn = _gdn_prompt(proj, bp, seq, off, n_heads=h_g, **lw["gdn"])
        xp = _tail(xp, y, o, proj, off, p_prompt[i].reshape(tp, -1), lw["tail"],
                   _pick_tile(tp, (256, 128)), last)
        proj3 = proj.reshape(bp, seq, npad)
        tail_rows = proj3[:, seq - (CONV_K - 1):, :]
        new_p[0].append(st_ssm.reshape(bp, h_m, P_M, N_M))
        new_p[1].append(tail_rows[:, :, off["xbc"]:off["xbc"] + conv_m])
        new_p[2].append(st_gdn)
        new_p[3].append(tail_rows[:, :, off["qkv"]:off["qkv"] + conv_g])

        projs = _inproj(xs, lw["norm_mix"], lw["w1"], _pick_tile(bs, (128, 64, 32, 16, 8)), tn)
        ys, cs_ssm, ss_ssm = _ssd_sample(projs, off, jnp.swapaxes(state_ssm_conv[i], 0, 1),
                                         state_ssm[i].reshape(bs, d_inner, N_M), **lw["ssm"])
        os_, cs_gdn, ss_gdn = _gdn_sample(projs, off, jnp.swapaxes(state_gdn_conv[i], 0, 1), state_gdn[i],
                                          **lw["gdn"])
        xs = _tail(xs, ys, os_, projs, off, p_sample[i].reshape(bs, -1), lw["tail"],
                   _pick_tile(bs, (128, 64, 32, 16, 8)), last)
        new_s[0].append(ss_ssm.reshape(bs, h_m, P_M, N_M))
        new_s[1].append(jnp.swapaxes(cs_ssm, 0, 1))
        new_s[2].append(ss_gdn)
        new_s[3].append(jnp.swapaxes(cs_gdn, 0, 1))

    stack = lambda lst: jnp.stack(lst)
    return (xp.reshape(bp, seq, d_model), xs.reshape(bs, 1, d_model),
            stack(new_p[0]), stack(new_p[1]), stack(new_p[2]), stack(new_p[3]),
            stack(new_s[0]), stack(new_s[1]), stack(new_s[2]), stack(new_s[3]))
```

```python
import functools

import jax
import jax.numpy as jnp
from jax import lax
from jax.experimental import pallas as pl
from jax.experimental.pallas import tpu as pltpu

f32 = jnp.float32
bf16 = jnp.bfloat16

EPS = 1e-6
CONV_K = 4
LANES = 128
SUBLANES = 8
VMEM_LIMIT = 56 * 1024 * 1024

P_M = 64
N_M = 128
G_M = 4
DK = 128
DV = 128
SSD_BLOCK = 128
GDN_CHUNK = 64
GDN_BLOCK = 128
SAMPLE_BB = 8


def _nt(a, b):
    return lax.dot_general(a, b, (((1,), (1,)), ((), ())), preferred_element_type=f32)


def _tn(a, b, precision=None):
    return lax.dot_general(a, b, (((0,), (0,)), ((), ())), preferred_element_type=f32, precision=precision)


def _mm(a, b):
    return jnp.dot(a, b, preferred_element_type=f32)


def _split3(x):
    hi = x.astype(bf16)
    r1 = x - hi.astype(f32)
    mid = r1.astype(bf16)
    lo = (r1 - mid.astype(f32)).astype(bf16)
    return hi, mid, lo


def _mm_sel_rhs(x, sel3):
    return _mm(jnp.concatenate(_split3(x), axis=1), sel3)


def _mm_sel_lhs(sel, x):
    return _mm(jnp.concatenate([sel] * 3, axis=1), jnp.concatenate(_split3(x), axis=0))


def _silu(x):
    return x * jax.nn.sigmoid(x)


def _softplus(x):
    return jnp.maximum(x, 0.0) + jnp.log1p(jnp.exp(-jnp.abs(x)))


def _rms(x, g):
    return x * lax.rsqrt(jnp.mean(x * x, axis=-1, keepdims=True) + EPS) * g


def _iota2(shape):
    return lax.broadcasted_iota(jnp.int32, shape, 0), lax.broadcasted_iota(jnp.int32, shape, 1)


def _inproj_kernel(x_ref, g_ref, w_ref, o_ref, h_ref):
    @pl.when(pl.program_id(1) == 0)
    def _():
        h_ref[...] = _rms(x_ref[...], g_ref[...]).astype(bf16)

    o_ref[...] = _mm(h_ref[...], w_ref[...])


def _inproj(x, g, w, tm, tn):
    t, d = x.shape
    npad = w.shape[1]
    return pl.pallas_call(
        _inproj_kernel,
        grid=(t // tm, npad // tn),
        in_specs=[
            pl.BlockSpec((tm, d), lambda i, j: (i, 0)),
            pl.BlockSpec((1, d), lambda i, j: (0, 0)),
            pl.BlockSpec((d, tn), lambda i, j: (0, j)),
        ],
        out_specs=pl.BlockSpec((tm, tn), lambda i, j: (i, j)),
        out_shape=jax.ShapeDtypeStruct((t, npad), f32),
        scratch_shapes=[pltpu.VMEM((tm, d), bf16)],
        compiler_params=pltpu.CompilerParams(
            dimension_semantics=("parallel", "arbitrary"), vmem_limit_bytes=VMEM_LIMIT),
        name="inproj",
    )(x, g, w)


def _conv_block(u_ref, ext_ref, act_ref, cw_ref, cb_ref, rows, width, chunk=512):
    for c0 in range(0, width, chunk):
        sl = slice(c0, c0 + chunk)
        u = u_ref[:, sl]
        ext_ref[SUBLANES:SUBLANES + rows, sl] = u
        cw = cw_ref[:, sl]
        conv = u * cw[3:4]
        for j in range(CONV_K - 1):
            off = SUBLANES - (CONV_K - 1) + j
            conv = conv + ext_ref[off:off + rows, sl] * cw[j:j + 1]
        if cb_ref is not None:
            conv = conv + cb_ref[:, sl]
        act_ref[:, sl] = _silu(conv)
        ext_ref[0:SUBLANES, sl] = ext_ref[rows:rows + SUBLANES, sl]


def _ssd_kernel(xbc_ref, z_ref, dt_ref, cw_ref, cb_ref, dtb_ref, aneg_ref, dexp_ref, nw_ref, e_ref,
                y_ref, st_out_ref, ext_ref, act_ref, st_ref, yd_ref):
    i = pl.program_id(1)
    c = SSD_BLOCK
    d_inner = y_ref.shape[1]
    gw = d_inner // G_M
    hg = gw // P_M

    @pl.when(i == 0)
    def _():
        ext_ref[0:SUBLANES, :] = jnp.zeros((SUBLANES, ext_ref.shape[1]), f32)
        st_ref[...] = jnp.zeros(st_ref.shape, f32)

    _conv_block(xbc_ref, ext_ref, act_ref, cw_ref, cb_ref, c, act_ref.shape[1])

    r, cc = _iota2((c, c))
    lower = r >= cc
    dt = _softplus(dt_ref[...] + dtb_ref[...])
    da = dt * aneg_ref[...]
    acum = _mm_sel_lhs(jnp.where(lower, 1.0, 0.0).astype(bf16), da)
    upper3 = jnp.concatenate([jnp.where(r <= cc, 1.0, 0.0).astype(bf16)] * 3, axis=0)
    acum_t = _mm_sel_rhs(da.T, upper3)
    alast = acum[c - 1:c, :]
    dout = jnp.exp(alast - acum)
    scal = jnp.concatenate(
        [dt, dt * dout, jnp.exp(acum), jnp.broadcast_to(jnp.exp(alast), (SUBLANES, LANES))], axis=0)

    for g in range(G_M):
        gs = slice(g * gw, (g + 1) * gw)
        se = _mm_sel_rhs(scal, e_ref[:, gs])
        dt_e, dd_e, ea_e, cd_e = se[0:c], se[c:2 * c], se[2 * c:3 * c], se[3 * c:3 * c + 1]
        xs = act_ref[:, gs]
        bm = act_ref[:, d_inner + g * N_M:d_inner + (g + 1) * N_M].astype(bf16)
        cm = act_ref[:, d_inner + (G_M + g) * N_M:d_inner + (G_M + g + 1) * N_M].astype(bf16)
        xdt = (xs * dt_e).astype(bf16)
        cb = _nt(cm, bm)
        st = st_ref[:, gs]
        y_off = _mm(cm, st.astype(bf16)) * ea_e
        st_ref[:, gs] = st * cd_e + _tn(bm, (xs * dd_e).astype(bf16))
        for hh in range(hg):
            h = g * hg + hh
            lmat = jnp.exp(jnp.where(lower, acum[:, h:h + 1] - acum_t[h:h + 1, :], -jnp.inf))
            yd_ref[:, hh * P_M:(hh + 1) * P_M] = _mm((cb * lmat).astype(bf16), xdt[:, hh * P_M:(hh + 1) * P_M])
        y = yd_ref[...] + y_off + xs * dexp_ref[:, gs]
        y = y * _silu(z_ref[:, gs])
        y_ref[:, gs] = _rms(y, nw_ref[:, gs]).astype(y_ref.dtype)

    @pl.when(i == pl.num_programs(1) - 1)
    def _():
        for j in range(d_inner // LANES):
            st_out_ref[0, j * LANES:(j + 1) * LANES, :] = st_ref[:, j * LANES:(j + 1) * LANES].T


def _ssd_prompt(proj, bsz, seq, off, cw, cb, dtb, aneg, dexp, nw, e_mat):
    d_inner = dexp.shape[1]
    conv_w = cw.shape[1]
    nb = seq // SSD_BLOCK
    c = SSD_BLOCK
    row = lambda b, i: b * nb + i
    const = lambda shape: pl.BlockSpec(shape, lambda b, i: (0, 0))
    return pl.pallas_call(
        _ssd_kernel,
        grid=(bsz, nb),
        in_specs=[
            pl.BlockSpec((c, conv_w), lambda b, i: (row(b, i), off["xbc"] // conv_w)),
            pl.BlockSpec((c, d_inner), lambda b, i: (row(b, i), off["z"] // d_inner)),
            pl.BlockSpec((c, LANES), lambda b, i: (row(b, i), off["dt"] // LANES)),
            const(cw.shape), const(cb.shape), const(dtb.shape), const(aneg.shape), const(dexp.shape),
            const(nw.shape), const(e_mat.shape),
        ],
        out_specs=[
            pl.BlockSpec((c, d_inner), lambda b, i: (row(b, i), 0)),
            pl.BlockSpec((1, d_inner, N_M), lambda b, i: (b, 0, 0)),
        ],
        out_shape=[
            jax.ShapeDtypeStruct((bsz * seq, d_inner), bf16),
            jax.ShapeDtypeStruct((bsz, d_inner, N_M), f32),
        ],
        scratch_shapes=[
            pltpu.VMEM((c + SUBLANES, conv_w), f32),
            pltpu.VMEM((c, conv_w), f32),
            pltpu.VMEM((N_M, d_inner), f32),
            pltpu.VMEM((c, d_inner // G_M), f32),
        ],
        compiler_params=pltpu.CompilerParams(
            dimension_semantics=("parallel", "arbitrary"), vmem_limit_bytes=VMEM_LIMIT),
        name="ssd_prompt",
    )(proj, proj, proj, cw, cb, dtb, aneg, dexp, nw, e_mat)


def _gdn_kernel(qkv_ref, gate_ref, b_ref, a_ref, cw_ref, dtb_ref, aneg_ref, nw_ref,
                o_ref, st_out_ref, ext_ref, act_ref, st_ref):
    i = pl.program_id(1)
    blk = GDN_BLOCK
    ck = GDN_CHUNK
    n_heads = st_ref.shape[0]
    hk = n_heads * DK

    @pl.when(i == 0)
    def _():
        ext_ref[0:SUBLANES, :] = jnp.zeros((SUBLANES, ext_ref.shape[1]), f32)
        st_ref[...] = jnp.zeros(st_ref.shape, f32)

    _conv_block(qkv_ref, ext_ref, act_ref, cw_ref, None, blk, act_ref.shape[1])

    n_chunks = blk // ck
    heads = range(n_heads)
    r, cc = _iota2((blk, blk))
    same = (r // ck) == (cc // ck)
    lower = same & (r >= cc)
    strict = same & (r > cc)
    beta = jax.nn.sigmoid(b_ref[...])
    glog = aneg_ref[...] * _softplus(a_ref[...] + dtb_ref[...])
    gcum = _mm_sel_lhs(jnp.where(lower, 1.0, 0.0).astype(bf16), glog)
    upper3 = jnp.concatenate([jnp.where(same & (r <= cc), 1.0, 0.0).astype(bf16)] * 3, axis=0)
    gcum_t = _mm_sel_rhs(glog.T, upper3)
    rcol = r[:, 0:1]

    nmats, xs, qks, qgs, kgs, gls = [], [], [], [], [], []
    for h in heads:
        q = act_ref[:, h * DK:(h + 1) * DK]
        k = act_ref[:, hk + h * DK:hk + (h + 1) * DK]
        v = act_ref[:, 2 * hk + h * DV:2 * hk + (h + 1) * DV]
        qn = q * lax.rsqrt(jnp.sum(q * q, axis=-1, keepdims=True) + EPS) * (DK ** -0.5)
        kn = k * lax.rsqrt(jnp.sum(k * k, axis=-1, keepdims=True) + EPS)
        bcol = beta[:, h:h + 1]
        gcol = gcum[:, h:h + 1]
        eg = jnp.exp(gcol)
        dm = jnp.exp(jnp.where(lower, gcol - gcum_t[h:h + 1, :], -jnp.inf))
        knb = kn.astype(bf16)
        kb = kn * bcol
        nmats.append(jnp.where(strict, -(_nt(kb.astype(bf16), knb) * dm), 0.0))
        xs.append(jnp.concatenate([v * bcol, kb * eg], axis=-1))
        qks.append(jnp.where(lower, _nt(qn.astype(bf16), knb) * dm, 0.0).astype(bf16))
        qgs.append((qn * eg).astype(bf16))
        glast = [gcol[(j + 1) * ck - 1:(j + 1) * ck] for j in range(n_chunks)]
        glast_row = glast[n_chunks - 1]
        for j in range(n_chunks - 2, -1, -1):
            glast_row = jnp.where(rcol < (j + 1) * ck, glast[j], glast_row)
        kgs.append((kn * jnp.exp(glast_row - gcol)).astype(bf16))
        gls.append([jnp.exp(g) for g in glast])

    span = 1
    while span < ck:
        nbs = [n.astype(bf16) for n in nmats]
        xs = [x + _mm(nb, x.astype(bf16)) for nb, x in zip(nbs, xs)]
        span *= 2
        if span < ck:
            nmats = [_mm(nb, nb) for nb in nbs]

    ss = [st_ref[h] for h in heads]
    for j in range(n_chunks):
        rs = slice(j * ck, (j + 1) * ck)
        sbs = [s.astype(bf16) for s in ss]
        vns = [xs[h][rs, :DV] - _mm(xs[h][rs, DV:].astype(bf16), sbs[h]) for h in heads]
        zeros = jnp.zeros((ck, DV), bf16)
        vfull = [jnp.concatenate([zeros] * j + [vns[h].astype(bf16)] + [zeros] * (n_chunks - 1 - j), axis=0)
                 for h in heads]
        outs = [_mm(qgs[h][rs], sbs[h]) + _mm(qks[h][rs, :], vfull[h]) for h in heads]
        ss = [ss[h] * gls[h][j] + _tn(kgs[h][rs], vns[h].astype(bf16)) for h in heads]
        for h in heads:
            gt = gate_ref[rs, h * DV:(h + 1) * DV]
            o_ref[rs, h * DV:(h + 1) * DV] = (_rms(outs[h], nw_ref[...]) * _silu(gt)).astype(o_ref.dtype)
    for h in heads:
        st_ref[h] = ss[h]

    @pl.when(i == pl.num_programs(1) - 1)
    def _():
        st_out_ref[0] = st_ref[...]


def _gdn_prompt(proj, bsz, seq, off, cw, dtb, aneg, nw, n_heads):
    conv_w = cw.shape[1]
    hv = n_heads * DV
    blk = GDN_BLOCK
    nb = seq // blk
    row = lambda b, i: b * nb + i
    const = lambda shape: pl.BlockSpec(shape, lambda b, i: (0, 0))
    return pl.pallas_call(
        _gdn_kernel,
        grid=(bsz, nb),
        in_specs=[
            pl.BlockSpec((blk, conv_w), lambda b, i: (row(b, i), off["qkv"] // conv_w)),
            pl.BlockSpec((blk, hv), lambda b, i: (row(b, i), off["gate"] // hv)),
            pl.BlockSpec((blk, LANES), lambda b, i: (row(b, i), off["b"] // LANES)),
            pl.BlockSpec((blk, LANES), lambda b, i: (row(b, i), off["a"] // LANES)),
            const(cw.shape), const(dtb.shape), const(aneg.shape), const(nw.shape),
        ],
        out_specs=[
            pl.BlockSpec((blk, hv), lambda b, i: (row(b, i), 0)),
            pl.BlockSpec((1, n_heads, DK, DV), lambda b, i: (b, 0, 0, 0)),
        ],
        out_shape=[
            jax.ShapeDtypeStruct((bsz * seq, hv), bf16),
            jax.ShapeDtypeStruct((bsz, n_heads, DK, DV), f32),
        ],
        scratch_shapes=[
            pltpu.VMEM((blk + SUBLANES, conv_w), f32),
            pltpu.VMEM((blk, conv_w), f32),
            pltpu.VMEM((n_heads, DK, DV), f32),
        ],
        compiler_params=pltpu.CompilerParams(
            dimension_semantics=("parallel", "arbitrary"), vmem_limit_bytes=VMEM_LIMIT),
        name="gdn_prompt",
    )(proj, proj, proj, proj, cw, dtb, aneg, nw)


def _conv_step(u_ref, cs_ref, cs_out_ref, cw_ref, cb_ref):
    u = u_ref[...]
    cw = cw_ref[...]
    conv = u * cw[3:4]
    for j in range(CONV_K - 1):
        conv = conv + cs_ref[j] * cw[j:j + 1]
    if cb_ref is not None:
        conv = conv + cb_ref[...]
    for j in range(CONV_K - 2):
        cs_out_ref[j] = cs_ref[j + 1]
    cs_out_ref[CONV_K - 2] = u
    return _silu(conv)


def _ssd_step_kernel(xbc_ref, z_ref, dt_ref, cs_ref, st_ref, cw_ref, cb_ref, dtb_ref, aneg_ref, dexp_ref,
                     nw_ref, e_ref, y_ref, cs_out_ref, st_out_ref, yacc_ref):
    bb = xbc_ref.shape[0]
    d_inner = y_ref.shape[1]
    gw = d_inner // G_M
    act = _conv_step(xbc_ref, cs_ref, cs_out_ref, cw_ref, cb_ref)
    dt = _softplus(dt_ref[...] + dtb_ref[...])
    dec = jnp.exp(dt * aneg_ref[...])
    se = _mm_sel_rhs(jnp.concatenate([dt, dec], axis=0), e_ref[...])
    xs = act[:, :d_inner]
    xdt = xs * se[0:bb]
    dec_e = se[bb:2 * bb]
    d3 = jnp.concatenate(_split3(dec_e), axis=0)
    ri, _ = _iota2((bb, LANES))
    r3, _ = _iota2((3 * bb, LANES))

    def body(b, carry):
        pick = ri == b
        ones3 = jnp.where((r3 == b) | (r3 == b + bb) | (r3 == b + 2 * bb), 1.0, 0.0).astype(bf16)
        for g in range(G_M):
            gs = slice(g * gw, (g + 1) * gw)
            bm = act[:, d_inner + g * N_M:d_inner + (g + 1) * N_M]
            cm = act[:, d_inner + (G_M + g) * N_M:d_inner + (G_M + g + 1) * N_M]
            dcol = _tn(d3[:, gs], ones3)
            outer = _tn(xdt[:, gs].astype(bf16), jnp.where(pick, bm, 0.0).astype(bf16))
            s_new = st_ref[b, gs, :] * dcol + outer
            st_out_ref[b, gs, :] = s_new
            yacc_ref[:, gs] = _nt(cm.astype(bf16), s_new.astype(bf16))
        return jnp.where(pick[:, 0:1], yacc_ref[...], carry)

    yssm = lax.fori_loop(0, bb, body, jnp.zeros((bb, d_inner), f32))
    y = yssm + xs * dexp_ref[...]
    y = y * _silu(z_ref[...])
    for g in range(G_M):
        gs = slice(g * gw, (g + 1) * gw)
        y_ref[:, gs] = _rms(y[:, gs], nw_ref[:, gs]).astype(y_ref.dtype)


def _ssd_sample(proj, off, conv_state, state, cw, cb, dtb, aneg, dexp, nw, e_mat):
    bsz = proj.shape[0]
    d_inner = dexp.shape[1]
    conv_w = cw.shape[1]
    bb = SAMPLE_BB
    const = lambda shape: pl.BlockSpec(shape, lambda i: (0, 0))
    return pl.pallas_call(
        _ssd_step_kernel,
        grid=(bsz // bb,),
        in_specs=[
            pl.BlockSpec((bb, conv_w), lambda i: (i, off["xbc"] // conv_w)),
            pl.BlockSpec((bb, d_inner), lambda i: (i, off["z"] // d_inner)),
            pl.BlockSpec((bb, LANES), lambda i: (i, off["dt"] // LANES)),
            pl.BlockSpec((CONV_K - 1, bb, conv_w), lambda i: (0, i, 0)),
            pl.BlockSpec((bb, d_inner, N_M), lambda i: (i, 0, 0)),
            const(cw.shape), const(cb.shape), const(dtb.shape), const(aneg.shape), const(dexp.shape),
            const(nw.shape), const(e_mat.shape),
        ],
        out_specs=[
            pl.BlockSpec((bb, d_inner), lambda i: (i, 0)),
            pl.BlockSpec((CONV_K - 1, bb, conv_w), lambda i: (0, i, 0)),
            pl.BlockSpec((bb, d_inner, N_M), lambda i: (i, 0, 0)),
        ],
        out_shape=[
            jax.ShapeDtypeStruct((bsz, d_inner), bf16),
            jax.ShapeDtypeStruct((CONV_K - 1, bsz, conv_w), f32),
            jax.ShapeDtypeStruct((bsz, d_inner, N_M), f32),
        ],
        scratch_shapes=[pltpu.VMEM((bb, d_inner), f32)],
        compiler_params=pltpu.CompilerParams(
            dimension_semantics=("parallel",), vmem_limit_bytes=VMEM_LIMIT),
        name="ssd_sample",
    )(proj, proj, proj, conv_state, state, cw, cb, dtb, aneg, dexp, nw, e_mat)


def _gdn_step_kernel(qkv_ref, gate_ref, b_ref, a_ref, cs_ref, st_ref, cw_ref, dtb_ref, aneg_ref, nw_ref,
                     o_ref, cs_out_ref, st_out_ref):
    bb = qkv_ref.shape[0]
    n_heads = st_ref.shape[1]
    hk = n_heads * DK
    act = _conv_step(qkv_ref, cs_ref, cs_out_ref, cw_ref, None)
    beta = jax.nn.sigmoid(b_ref[...])
    eg = jnp.exp(aneg_ref[...] * _softplus(a_ref[...] + dtb_ref[...]))
    ri, _ = _iota2((bb, DV))

    heads = []
    for h in range(n_heads):
        q = act[:, h * DK:(h + 1) * DK]
        k = act[:, hk + h * DK:hk + (h + 1) * DK]
        v = act[:, 2 * hk + h * DV:2 * hk + (h + 1) * DV]
        qn = q * lax.rsqrt(jnp.sum(q * q, axis=-1, keepdims=True) + EPS) * (DK ** -0.5)
        kn = k * lax.rsqrt(jnp.sum(k * k, axis=-1, keepdims=True) + EPS)
        bcol = beta[:, h:h + 1]
        ecol = eg[:, h:h + 1]
        kb = kn * bcol
        heads.append(dict(
            u=v * bcol, w=(kb * ecol).astype(bf16), qg=(qn * ecol).astype(bf16),
            qk=jnp.sum(qn.astype(bf16).astype(f32) * kn.astype(bf16).astype(f32), axis=-1, keepdims=True),
            kn=kn.astype(bf16), ecol=ecol))

    def body(b, carry):
        pick = ri == b
        outs = []
        for h in range(n_heads):
            hd = heads[h]
            s = st_ref[b, h]
            sb = s.astype(bf16)
            vn = hd["u"] - _mm(hd["w"], sb)
            o = _mm(hd["qg"], sb) + hd["qk"] * vn
            vn_b = jnp.where(pick, vn, 0.0).astype(bf16)
            gl = jnp.sum(jnp.where(pick[:, 0:1], hd["ecol"], 0.0), axis=0, keepdims=True)
            st_out_ref[b, h] = s * gl + _tn(hd["kn"], vn_b)
            outs.append(jnp.where(pick, o, carry[h]))
        return tuple(outs)

    init = tuple(jnp.zeros((bb, DV), f32) for _ in range(n_heads))
    outs = lax.fori_loop(0, bb, body, init)
    for h in range(n_heads):
        gt = gate_ref[:, h * DV:(h + 1) * DV]
        o_ref[:, h * DV:(h + 1) * DV] = (_rms(outs[h], nw_ref[...]) * _silu(gt)).astype(o_ref.dtype)


def _gdn_sample(proj, off, conv_state, state, cw, dtb, aneg, nw):
    bsz = proj.shape[0]
    n_heads = state.shape[1]
    conv_w = cw.shape[1]
    hv = n_heads * DV
    bb = SAMPLE_BB
    const = lambda shape: pl.BlockSpec(shape, lambda i: (0, 0))
    return pl.pallas_call(
        _gdn_step_kernel,
        grid=(bsz // bb,),
        in_specs=[
            pl.BlockSpec((bb, conv_w), lambda i: (i, off["qkv"] // conv_w)),
            pl.BlockSpec((bb, hv), lambda i: (i, off["gate"] // hv)),
            pl.BlockSpec((bb, LANES), lambda i: (i, off["b"] // LANES)),
            pl.BlockSpec((bb, LANES), lambda i: (i, off["a"] // LANES)),
            pl.BlockSpec((CONV_K - 1, bb, conv_w), lambda i: (0, i, 0)),
            pl.BlockSpec((bb, n_heads, DK, DV), lambda i: (i, 0, 0, 0)),
            const(cw.shape), const(dtb.shape), const(aneg.shape), const(nw.shape),
        ],
        out_specs=[
            pl.BlockSpec((bb, hv), lambda i: (i, 0)),
            pl.BlockSpec((CONV_K - 1, bb, conv_w), lambda i: (0, i, 0)),
            pl.BlockSpec((bb, n_heads, DK, DV), lambda i: (i, 0, 0, 0)),
        ],
        out_shape=[
            jax.ShapeDtypeStruct((bsz, hv), bf16),
            jax.ShapeDtypeStruct((CONV_K - 1, bsz, conv_w), f32),
            jax.ShapeDtypeStruct((bsz, n_heads, DK, DV), f32),
        ],
        compiler_params=pltpu.CompilerParams(
            dimension_semantics=("parallel",), vmem_limit_bytes=VMEM_LIMIT),
        name="gdn_sample",
    )(proj, proj, proj, proj, conv_state, state, cw, dtb, aneg, nw)


def _tail_kernel(x_ref, y_ref, o_ref, mg_ref, p_ref, wbs_ref, wbg_ref, wout_ref, nf_ref, wfi_ref, wfo_ref,
                 npl_ref, wpg_ref, wpp_ref, nfin_ref, out_ref, *, ff_chunks, final_norm):
    d = x_ref.shape[1]
    d_ff = wfo_ref.shape[0]
    mg = mg_ref[...]
    mix = (jax.nn.sigmoid(mg[:, :d]) * _mm(y_ref[...], wbs_ref[...])
           + jax.nn.sigmoid(mg[:, d:]) * _mm(o_ref[...], wbg_ref[...]))
    x = x_ref[...] + _mm(mix.astype(bf16), wout_ref[...])
    h = _rms(x, nf_ref[...]).astype(bf16)
    fc = d_ff // ff_chunks
    for c in range(ff_chunks):
        gt = _mm(h, wfi_ref[:, c * fc:(c + 1) * fc])
        up = _mm(h, wfi_ref[:, d_ff + c * fc:d_ff + (c + 1) * fc])
        x = x + _mm((_silu(gt) * up).astype(bf16), wfo_ref[c * fc:(c + 1) * fc, :])
    pe = _mm(p_ref[...].astype(bf16), wpp_ref[...])
    x = x + pe * jax.nn.sigmoid(_mm(_rms(x, npl_ref[...]).astype(bf16), wpg_ref[...]))
    if final_norm:
        x = _rms(x, nfin_ref[...])
    out_ref[...] = x


def _tail(x, y, o, proj, off, p, wts, tm, final_norm):
    t, d = x.shape
    tok = lambda w: pl.BlockSpec((tm, w), lambda i: (i, 0))
    res = lambda a: pl.BlockSpec(a.shape, lambda i: (0, 0), pipeline_mode=pl.Buffered(1))
    names = ("wbs", "wbg", "wout", "nf", "wfi", "wfo", "npl", "wpg", "wpp", "nfin")
    return pl.pallas_call(
        functools.partial(_tail_kernel, ff_chunks=2, final_norm=final_norm),
        grid=(t // tm,),
        in_specs=[tok(d), tok(y.shape[1]), tok(o.shape[1]),
                  pl.BlockSpec((tm, 2 * d), lambda i: (i, off["mgate"] // (2 * d))),
                  tok(p.shape[1])] + [res(wts[n]) for n in names],
        out_specs=tok(d),
        out_shape=jax.ShapeDtypeStruct((t, d), f32),
        compiler_params=pltpu.CompilerParams(
            dimension_semantics=("parallel",), vmem_limit_bytes=VMEM_LIMIT),
        name="tail",
    )(x, y, o, proj, p, *[wts[n] for n in names])


def _pad_lanes(v, width=LANES):
    return jnp.pad(v, ((0, 0), (0, width - v.shape[1])))


def _layer_weights(i, d_model, norm_mix, w_in, ssm_conv_w, ssm_conv_b, ssm_dt_bias, ssm_a_log, ssm_d, ssm_norm,
                   gdn_conv_w, gdn_dt_bias, gdn_a_log, gdn_norm, w_branch_ssm, w_branch_gdn, w_out,
                   norm_ffn, w_ffn_in, w_ffn_out, norm_pl, w_pl_gate, w_pl_proj, norm_final):
    h_m = ssm_dt_bias.shape[1]
    h_g = gdn_dt_bias.shape[1]
    d_inner = h_m * P_M
    conv_m = ssm_conv_w.shape[2]
    conv_g = gdn_conv_w.shape[2]
    sizes = (d_inner, conv_m, h_m, conv_g, h_g * DV, h_g, h_g, 2 * d_model)
    starts = [0]
    for s in sizes:
        starts.append(starts[-1] + s)
    seg = lambda j: w_in[i][:, starts[j]:starts[j + 1]]
    z, xbc, dtw, qkv, gate, bw, aw, mg = (seg(j) for j in range(8))
    order = [("xbc", xbc), ("qkv", qkv), ("z", z), ("mgate", mg), ("gate", gate),
             ("dt", _pad_lanes(dtw)), ("b", _pad_lanes(bw)), ("a", _pad_lanes(aw))]
    off, pos = {}, 0
    for name, w in order:
        off[name] = pos
        pos += w.shape[1]
    w1 = jnp.concatenate([w for _, w in order], axis=1).astype(bf16)
    head_of = jnp.arange(d_inner, dtype=jnp.int32) // P_M
    e_mat = (jnp.arange(LANES, dtype=jnp.int32)[:, None] == head_of[None, :]).astype(bf16)
    e_mat = jnp.concatenate([e_mat] * 3, axis=0)
    row = lambda v: v.reshape(1, -1).astype(f32)
    return dict(
        off=off, w1=w1, norm_mix=row(norm_mix[i]),
        ssm=dict(cw=ssm_conv_w[i], cb=row(ssm_conv_b[i]), dtb=_pad_lanes(row(ssm_dt_bias[i])),
                 aneg=_pad_lanes(-jnp.exp(row(ssm_a_log[i]))), dexp=row(jnp.repeat(ssm_d[i], P_M)),
                 nw=row(ssm_norm[i]), e_mat=e_mat),
        gdn=dict(cw=gdn_conv_w[i], dtb=_pad_lanes(row(gdn_dt_bias[i])),
                 aneg=_pad_lanes(-jnp.exp(row(gdn_a_log[i]))), nw=row(gdn_norm[i])),
        tail=dict(wbs=w_branch_ssm[i].astype(bf16), wbg=w_branch_gdn[i].astype(bf16), wout=w_out[i].astype(bf16),
                  nf=row(norm_ffn[i]), wfi=w_ffn_in[i].astype(bf16), wfo=w_ffn_out[i].astype(bf16),
                  npl=row(norm_pl[i]), wpg=w_pl_gate[i].astype(bf16), wpp=w_pl_proj[i].astype(bf16),
                  nfin=row(norm_final)),
        h_g=h_g,
    )


def _pick_tile(n, candidates):
    for c in candidates:
        if n % c == 0:
            return c
    raise ValueError(f"no tile for {n}")


def kernel(x_prompt, x_sample, p_prompt, p_sample, state_ssm, state_ssm_conv, state_gdn, state_gdn_conv,
           norm_mix, w_in, ssm_conv_w, ssm_conv_b, ssm_dt_bias, ssm_a_log, ssm_d, ssm_norm,
           gdn_conv_w, gdn_dt_bias, gdn_a_log, gdn_norm, w_branch_ssm, w_branch_gdn, w_out,
           norm_ffn, w_ffn_in, w_ffn_out, norm_pl, w_pl_gate, w_pl_proj, norm_final):
    depth = p_prompt.shape[0]
    bp, seq, d_model = x_prompt.shape
    bs, dec_seq, _ = x_sample.shape
    assert dec_seq == 1 and seq % SSD_BLOCK == 0 and seq % GDN_BLOCK == 0 and bs % SAMPLE_BB == 0
    h_m, h_g = ssm_dt_bias.shape[1], gdn_dt_bias.shape[1]
    d_inner = h_m * P_M
    conv_m, conv_g = ssm_conv_w.shape[2], gdn_conv_w.shape[2]

    xp = x_prompt.reshape(bp * seq, d_model)
    xs = x_sample.reshape(bs, d_model)
    new_p = ([], [], [], [])
    new_s = ([], [], [], [])
    for i in range(depth):
        lw = _layer_weights(i, d_model, norm_mix, w_in, ssm_conv_w, ssm_conv_b, ssm_dt_bias, ssm_a_log, ssm_d,
                            ssm_norm, gdn_conv_w, gdn_dt_bias, gdn_a_log, gdn_norm, w_branch_ssm, w_branch_gdn,
                            w_out, norm_ffn, w_ffn_in, w_ffn_out, norm_pl, w_pl_gate, w_pl_proj, norm_final)
        off = lw["off"]
        npad = lw["w1"].shape[1]
        tn = _pick_tile(npad, (1664, 896, 128))
        last = i == depth - 1

        tp = bp * seq
        proj = _inproj(xp, lw["norm_mix"], lw["w1"], _pick_tile(tp, (1024, 512, 256, 128)), tn)
        y, st_ssm = _ssd_prompt(proj, bp, seq, off, **lw["ssm"])
        o, st_gdn = _gdn_prompt(proj, bp, seq, off, n_heads=h_g, **lw["gdn"])
        xp = _tail(xp, y, o, proj, off, p_prompt[i].reshape(tp, -1), lw["tail"],
                   _pick_tile(tp, (256, 128)), last)
        proj3 = proj.reshape(bp, seq, npad)
        tail_rows = proj3[:, seq - (CONV_K - 1):, :]
        new_p[0].append(st_ssm.reshape(bp, h_m, P_M, N_M))
        new_p[1].append(tail_rows[:, :, off["xbc"]:off["xbc"] + conv_m])
        new_p[2].append(st_gdn)
        new_p[3].append(tail_rows[:, :, off["qkv"]:off["qkv"] + conv_g])

        projs = _inproj(xs, lw["norm_mix"], lw["w1"], _pick_tile(bs, (128, 64, 32, 16, 8)), tn)
        ys, cs_ssm, ss_ssm = _ssd_sample(projs, off, jnp.swapaxes(state_ssm_conv[i], 0, 1),
                                         state_ssm[i].reshape(bs, d_inner, N_M), **lw["ssm"])
        os_, cs_gdn, ss_gdn = _gdn_sample(projs, off, jnp.swapaxes(state_gdn_conv[i], 0, 1), state_gdn[i],
                                          **lw["gdn"])
        xs = _tail(xs, ys, os_, projs, off, p_sample[i].reshape(bs, -1), lw["tail"],
                   _pick_tile(bs, (128, 64, 32, 16, 8)), last)
        new_s[0].append(ss_ssm.reshape(bs, h_m, P_M, N_M))
        new_s[1].append(jnp.swapaxes(cs_ssm, 0, 1))
        new_s[2].append(ss_gdn)
        new_s[3].append(jnp.swapaxes(cs_gdn, 0, 1))

    stack = lambda lst: jnp.stack(lst)
    return (xp.reshape(bp, seq, d_model), xs.reshape(bs, 1, d_model),
            stack(new_p[0]), stack(new_p[1]), stack(new_p[2]), stack(new_p[3]),
            stack(new_s[0]), stack(new_s[1]), stack(new_s[2]), stack(new_s[3]))
```

```python
import functools

import jax
import jax.numpy as jnp
from jax import lax
from jax.experimental import pallas as pl
from jax.experimental.pallas import tpu as pltpu

f32 = jnp.float32
bf16 = jnp.bfloat16

EPS = 1e-6
CONV_K = 4
LANES = 128
SUBLANES = 8
VMEM_LIMIT = 56 * 1024 * 1024

P_M = 64
N_M = 128
G_M = 4
DK = 128
DV = 128
SSD_BLOCK = 128
GDN_CHUNK = 64
GDN_BLOCK = 128
STEP_TOKENS = 256
CONV_COLS = 512
SAMPLE_BB = 8


def _nt(a, b):
    return lax.dot_general(a, b, (((1,), (1,)), ((), ())), preferred_element_type=f32)


def _tn(a, b):
    return lax.dot_general(a, b, (((0,), (0,)), ((), ())), preferred_element_type=f32)


def _mm(a, b):
    return jnp.dot(a, b, preferred_element_type=f32)


def _split3(x):
    hi = x.astype(bf16)
    r1 = x - hi.astype(f32)
    mid = r1.astype(bf16)
    lo = (r1 - mid.astype(f32)).astype(bf16)
    return hi, mid, lo


def _mm_sel_rhs(x, sel3):
    return _mm(jnp.concatenate(_split3(x), axis=1), sel3)


def _mm_sel_lhs(sel, x):
    return _mm(jnp.concatenate([sel] * 3, axis=1), jnp.concatenate(_split3(x), axis=0))


def _silu(x):
    return x * jax.nn.sigmoid(x)


def _softplus(x):
    return jnp.maximum(x, 0.0) + jnp.log1p(jnp.exp(-jnp.abs(x)))


def _rms(x, g):
    return x * lax.rsqrt(jnp.mean(x * x, axis=-1, keepdims=True) + EPS) * g


def _iota2(shape):
    return lax.broadcasted_iota(jnp.int32, shape, 0), lax.broadcasted_iota(jnp.int32, shape, 1)


def _inproj_kernel(x_ref, g_ref, w_ref, o_ref):
    o_ref[...] = _mm(_rms(x_ref[...], g_ref[...]).astype(bf16), w_ref[...])


def _inproj(x, g, w):
    t, d = x.shape
    n = w.shape[1]
    full = lambda shape: pl.BlockSpec(shape, lambda i: (0, 0))
    return pl.pallas_call(
        _inproj_kernel,
        grid=(1,),
        in_specs=[full((t, d)), full((1, d)), full((d, n))],
        out_specs=full((t, n)),
        out_shape=jax.ShapeDtypeStruct((t, n), f32),
        compiler_params=pltpu.CompilerParams(dimension_semantics=("arbitrary",), vmem_limit_bytes=VMEM_LIMIT),
        name="inproj",
    )(x, g, w)


def _project_conv(h, w_ref, ext_ref, act_ref, cw_ref, cb_ref, rows, width):
    for c0 in range(0, width, CONV_COLS):
        sl = slice(c0, c0 + CONV_COLS)
        u = _mm(h, w_ref[:, sl])
        ext_ref[SUBLANES:SUBLANES + rows, sl] = u
        cw = cw_ref[:, sl]
        conv = u * cw[3:4]
        for j in range(CONV_K - 1):
            off = SUBLANES - (CONV_K - 1) + j
            conv = conv + ext_ref[off:off + rows, sl] * cw[j:j + 1]
        if cb_ref is not None:
            conv = conv + cb_ref[:, sl]
        act_ref[:, sl] = _silu(conv)
        ext_ref[0:SUBLANES, sl] = ext_ref[rows:rows + SUBLANES, sl]


def _ssd_kernel(x_ref, g_ref, w_ref, cw_ref, cb_ref, dtb_ref, aneg_ref, dexp_ref, nw_ref, e_ref,
                y_ref, st_out_ref, cs_out_ref, ext_ref, act_ref, z_ref, st_ref, yd_ref):
    i = pl.program_id(1)
    rows = x_ref.shape[0]
    c = SSD_BLOCK
    conv_w = act_ref.shape[1]
    d_inner = y_ref.shape[1]
    gw = d_inner // G_M
    hg = gw // P_M

    @pl.when(i == 0)
    def _():
        ext_ref[0:SUBLANES, :] = jnp.zeros((SUBLANES, conv_w), f32)
        st_ref[...] = jnp.zeros(st_ref.shape, f32)

    h = _rms(x_ref[...], g_ref[...]).astype(bf16)
    _project_conv(h, w_ref, ext_ref, act_ref, cw_ref, cb_ref, rows, conv_w)
    dt_raw = _mm(h, w_ref[:, conv_w + d_inner:])
    for g in range(G_M):
        z_ref[:, g * gw:(g + 1) * gw] = _mm(h, w_ref[:, conv_w + g * gw:conv_w + (g + 1) * gw])

    r, cc = _iota2((c, c))
    lower = r >= cc
    lower_b = jnp.where(lower, 1.0, 0.0).astype(bf16)
    upper3 = jnp.concatenate([jnp.where(r <= cc, 1.0, 0.0).astype(bf16)] * 3, axis=0)

    for sub in range(rows // c):
        rs = slice(sub * c, (sub + 1) * c)
        dt = _softplus(dt_raw[rs] + dtb_ref[...])
        da = dt * aneg_ref[...]
        acum = _mm_sel_lhs(lower_b, da)
        acum_t = _mm_sel_rhs(da.T, upper3)
        alast = acum[c - 1:c, :]
        dout = jnp.exp(alast - acum)
        scal = jnp.concatenate(
            [dt, dt * dout, jnp.exp(acum), jnp.broadcast_to(jnp.exp(alast), (SUBLANES, LANES))], axis=0)
        for g in range(G_M):
            gs = slice(g * gw, (g + 1) * gw)
            se = _mm_sel_rhs(scal, e_ref[:, gs])
            dt_e, dd_e, ea_e, cd_e = se[0:c], se[c:2 * c], se[2 * c:3 * c], se[3 * c:3 * c + 1]
            xs = act_ref[rs, gs]
            bm = act_ref[rs, d_inner + g * N_M:d_inner + (g + 1) * N_M].astype(bf16)
            cm = act_ref[rs, d_inner + (G_M + g) * N_M:d_inner + (G_M + g + 1) * N_M].astype(bf16)
            xdt = (xs * dt_e).astype(bf16)
            cb = _nt(cm, bm)
            st = st_ref[:, gs]
            y_off = _mm(cm, st.astype(bf16)) * ea_e
            st_ref[:, gs] = st * cd_e + _tn(bm, (xs * dd_e).astype(bf16))
            for hh in range(hg):
                hd = g * hg + hh
                lmat = jnp.exp(jnp.where(lower, acum[:, hd:hd + 1] - acum_t[hd:hd + 1, :], -jnp.inf))
                yd_ref[:, hh * P_M:(hh + 1) * P_M] = _mm((cb * lmat).astype(bf16), xdt[:, hh * P_M:(hh + 1) * P_M])
            y = yd_ref[...] + y_off + xs * dexp_ref[:, gs]
            y = y * _silu(z_ref[rs, gs])
            y_ref[rs, gs] = _rms(y, nw_ref[:, gs]).astype(y_ref.dtype)

    @pl.when(i == pl.num_programs(1) - 1)
    def _():
        cs_out_ref[0] = ext_ref[0:SUBLANES, :]
        for j in range(d_inner // LANES):
            st_out_ref[0, j * LANES:(j + 1) * LANES, :] = st_ref[:, j * LANES:(j + 1) * LANES].T


def _ssd_prompt(x, g, w, bsz, seq, cw, cb, dtb, aneg, dexp, nw, e_mat):
    d_model = x.shape[1]
    d_inner = dexp.shape[1]
    conv_w = cw.shape[1]
    rows = STEP_TOKENS
    nb = seq // rows
    tok = lambda b, i: (b * nb + i, 0)
    res = lambda a: pl.BlockSpec(a.shape, lambda b, i: (0, 0), pipeline_mode=pl.Buffered(1))
    return pl.pallas_call(
        _ssd_kernel,
        grid=(bsz, nb),
        in_specs=[pl.BlockSpec((rows, d_model), tok)] + [res(a) for a in (g, w, cw, cb, dtb, aneg, dexp, nw, e_mat)],
        out_specs=[
            pl.BlockSpec((rows, d_inner), tok),
            pl.BlockSpec((1, d_inner, N_M), lambda b, i: (b, 0, 0)),
            pl.BlockSpec((1, SUBLANES, conv_w), lambda b, i: (b, 0, 0)),
        ],
        out_shape=[
            jax.ShapeDtypeStruct((bsz * seq, d_inner), bf16),
            jax.ShapeDtypeStruct((bsz, d_inner, N_M), f32),
            jax.ShapeDtypeStruct((bsz, SUBLANES, conv_w), f32),
        ],
        scratch_shapes=[
            pltpu.VMEM((rows + SUBLANES, conv_w), f32),
            pltpu.VMEM((rows, conv_w), f32),
            pltpu.VMEM((rows, d_inner), f32),
            pltpu.VMEM((N_M, d_inner), f32),
            pltpu.VMEM((SSD_BLOCK, d_inner // G_M), f32),
        ],
        compiler_params=pltpu.CompilerParams(
            dimension_semantics=("parallel", "arbitrary"), vmem_limit_bytes=VMEM_LIMIT),
        name="ssd_prompt",
    )(x, g, w, cw, cb, dtb, aneg, dexp, nw, e_mat)


def _gdn_kernel(x_ref, g_ref, w_ref, cw_ref, dtb_ref, aneg_ref, nw_ref,
                o_ref, st_out_ref, cs_out_ref, ext_ref, act_ref, gate_ref, st_ref):
    i = pl.program_id(1)
    rows = x_ref.shape[0]
    blk = GDN_BLOCK
    ck = GDN_CHUNK
    conv_w = act_ref.shape[1]
    n_heads = st_ref.shape[0]
    hk = n_heads * DK
    hv = n_heads * DV
    per_blk = blk // ck
    heads = range(n_heads)

    @pl.when(i == 0)
    def _():
        ext_ref[0:SUBLANES, :] = jnp.zeros((SUBLANES, conv_w), f32)
        st_ref[...] = jnp.zeros(st_ref.shape, f32)

    h_in = _rms(x_ref[...], g_ref[...]).astype(bf16)
    _project_conv(h_in, w_ref, ext_ref, act_ref, cw_ref, None, rows, conv_w)
    ba = _mm(h_in, w_ref[:, conv_w + hv:])
    gate_ref[...] = _mm(h_in, w_ref[:, conv_w:conv_w + hv])

    r, cc = _iota2((blk, blk))
    same = (r // ck) == (cc // ck)
    lower = same & (r >= cc)
    strict = same & (r > cc)
    lower_b = jnp.where(lower, 1.0, 0.0).astype(bf16)
    upper3 = jnp.concatenate([jnp.where(same & (r <= cc), 1.0, 0.0).astype(bf16)] * 3, axis=0)
    rcol = r[:, 0:1]

    xs, qks, qgs, kgs, gls = [], [], [], [], []
    for sub in range(rows // blk):
        rs = slice(sub * blk, (sub + 1) * blk)
        beta = jax.nn.sigmoid(ba[rs, :LANES])
        glog = aneg_ref[...] * _softplus(ba[rs, LANES:] + dtb_ref[...])
        gcum = _mm_sel_lhs(lower_b, glog)
        gcum_t = _mm_sel_rhs(glog.T, upper3)
        nmats, xsub = [], []
        for h in heads:
            q = act_ref[rs, h * DK:(h + 1) * DK]
            k = act_ref[rs, hk + h * DK:hk + (h + 1) * DK]
            v = act_ref[rs, 2 * hk + h * DV:2 * hk + (h + 1) * DV]
            qn = q * lax.rsqrt(jnp.sum(q * q, axis=-1, keepdims=True) + EPS) * (DK ** -0.5)
            kn = k * lax.rsqrt(jnp.sum(k * k, axis=-1, keepdims=True) + EPS)
            bcol = beta[:, h:h + 1]
            gcol = gcum[:, h:h + 1]
            eg = jnp.exp(gcol)
            dm = jnp.exp(jnp.where(lower, gcol - gcum_t[h:h + 1, :], -jnp.inf))
            knb = kn.astype(bf16)
            kb = kn * bcol
            nmats.append(jnp.where(strict, -(_nt(kb.astype(bf16), knb) * dm), 0.0))
            xsub.append(jnp.concatenate([v * bcol, kb * eg], axis=-1))
            qks.append(jnp.where(lower, _nt(qn.astype(bf16), knb) * dm, 0.0).astype(bf16))
            qgs.append((qn * eg).astype(bf16))
            glast = [gcol[(j + 1) * ck - 1:(j + 1) * ck] for j in range(per_blk)]
            glast_row = glast[per_blk - 1]
            for j in range(per_blk - 2, -1, -1):
                glast_row = jnp.where(rcol < (j + 1) * ck, glast[j], glast_row)
            kgs.append((kn * jnp.exp(glast_row - gcol)).astype(bf16))
            gls.append([jnp.exp(gl) for gl in glast])
        span = 1
        while span < ck:
            nbs = [n.astype(bf16) for n in nmats]
            xsub = [x + _mm(nb, x.astype(bf16)) for nb, x in zip(nbs, xsub)]
            span *= 2
            if span < ck:
                nmats = [_mm(nb, nb) for nb in nbs]
        xs.extend(xsub)

    ss = [st_ref[h] for h in heads]
    zeros = jnp.zeros((ck, DV), bf16)
    for jc in range(rows // ck):
        sub, j = divmod(jc, per_blk)
        rl = slice(j * ck, (j + 1) * ck)
        ro = slice(jc * ck, (jc + 1) * ck)
        at = lambda lst, h: lst[sub * n_heads + h]
        sbs = [s.astype(bf16) for s in ss]
        vns = [at(xs, h)[rl, :DV] - _mm(at(xs, h)[rl, DV:].astype(bf16), sbs[h]) for h in heads]
        vfull = [jnp.concatenate([zeros] * j + [vns[h].astype(bf16)] + [zeros] * (per_blk - 1 - j), axis=0)
                 for h in heads]
        outs = [_mm(at(qgs, h)[rl], sbs[h]) + _mm(at(qks, h)[rl, :], vfull[h]) for h in heads]
        ss = [ss[h] * at(gls, h)[j] + _tn(at(kgs, h)[rl], vns[h].astype(bf16)) for h in heads]
        for h in heads:
            gt = gate_ref[ro, h * DV:(h + 1) * DV]
            o_ref[ro, h * DV:(h + 1) * DV] = (_rms(outs[h], nw_ref[...]) * _silu(gt)).astype(o_ref.dtype)
    for h in heads:
        st_ref[h] = ss[h]

    @pl.when(i == pl.num_programs(1) - 1)
    def _():
        cs_out_ref[0] = ext_ref[0:SUBLANES, :]
        st_out_ref[0] = st_ref[...]


def _gdn_prompt(x, g, w, bsz, seq, cw, dtb, aneg, nw, n_heads):
    d_model = x.shape[1]
    conv_w = cw.shape[1]
    hv = n_heads * DV
    rows = STEP_TOKENS
    nb = seq // rows
    tok = lambda b, i: (b * nb + i, 0)
    res = lambda a: pl.BlockSpec(a.shape, lambda b, i: (0, 0), pipeline_mode=pl.Buffered(1))
    return pl.pallas_call(
        _gdn_kernel,
        grid=(bsz, nb),
        in_specs=[pl.BlockSpec((rows, d_model), tok)] + [res(a) for a in (g, w, cw, dtb, aneg, nw)],
        out_specs=[
            pl.BlockSpec((rows, hv), tok),
            pl.BlockSpec((1, n_heads, DK, DV), lambda b, i: (b, 0, 0, 0)),
            pl.BlockSpec((1, SUBLANES, conv_w), lambda b, i: (b, 0, 0)),
        ],
        out_shape=[
            jax.ShapeDtypeStruct((bsz * seq, hv), bf16),
            jax.ShapeDtypeStruct((bsz, n_heads, DK, DV), f32),
            jax.ShapeDtypeStruct((bsz, SUBLANES, conv_w), f32),
        ],
        scratch_shapes=[
            pltpu.VMEM((rows + SUBLANES, conv_w), f32),
            pltpu.VMEM((rows, conv_w), f32),
            pltpu.VMEM((rows, hv), f32),
            pltpu.VMEM((n_heads, DK, DV), f32),
        ],
        compiler_params=pltpu.CompilerParams(
            dimension_semantics=("parallel", "arbitrary"), vmem_limit_bytes=VMEM_LIMIT),
        name="gdn_prompt",
    )(x, g, w, cw, dtb, aneg, nw)


def _conv_step(u, cs_ref, cs_out_ref, cw_ref, cb_ref):
    cw = cw_ref[...]
    conv = u * cw[3:4]
    for j in range(CONV_K - 1):
        conv = conv + cs_ref[j] * cw[j:j + 1]
    if cb_ref is not None:
        conv = conv + cb_ref[...]
    for j in range(CONV_K - 2):
        cs_out_ref[j] = cs_ref[j + 1]
    cs_out_ref[CONV_K - 2] = u
    return _silu(conv)


def _ssd_step_kernel(pr_ref, cs_ref, st_ref, cw_ref, cb_ref, dtb_ref, aneg_ref, dexp_ref,
                     nw_ref, e_ref, y_ref, cs_out_ref, st_out_ref, yacc_ref):
    bb = pr_ref.shape[0]
    d_inner = y_ref.shape[1]
    conv_w = cw_ref.shape[1]
    gw = d_inner // G_M
    act = _conv_step(pr_ref[:, :conv_w], cs_ref, cs_out_ref, cw_ref, cb_ref)
    z = pr_ref[:, conv_w:conv_w + d_inner]
    dt = _softplus(pr_ref[:, conv_w + d_inner:] + dtb_ref[...])
    dec = jnp.exp(dt * aneg_ref[...])
    se = _mm_sel_rhs(jnp.concatenate([dt, dec], axis=0), e_ref[...])
    xs = act[:, :d_inner]
    xdt = xs * se[0:bb]
    dec_e = se[bb:2 * bb]
    d3 = jnp.concatenate(_split3(dec_e), axis=0)
    ri, _ = _iota2((bb, LANES))
    r3, _ = _iota2((3 * bb, LANES))

    def body(b, carry):
        pick = ri == b
        ones3 = jnp.where((r3 == b) | (r3 == b + bb) | (r3 == b + 2 * bb), 1.0, 0.0).astype(bf16)
        for g in range(G_M):
            gs = slice(g * gw, (g + 1) * gw)
            bm = act[:, d_inner + g * N_M:d_inner + (g + 1) * N_M]
            cm = act[:, d_inner + (G_M + g) * N_M:d_inner + (G_M + g + 1) * N_M]
            dcol = _tn(d3[:, gs], ones3)
            outer = _tn(xdt[:, gs].astype(bf16), jnp.where(pick, bm, 0.0).astype(bf16))
            s_new = st_ref[b, gs, :] * dcol + outer
            st_out_ref[b, gs, :] = s_new
            yacc_ref[:, gs] = _nt(cm.astype(bf16), s_new.astype(bf16))
        return jnp.where(pick[:, 0:1], yacc_ref[...], carry)

    yssm = lax.fori_loop(0, bb, body, jnp.zeros((bb, d_inner), f32))
    y = yssm + xs * dexp_ref[...]
    y = y * _silu(z)
    for g in range(G_M):
        gs = slice(g * gw, (g + 1) * gw)
        y_ref[:, gs] = _rms(y[:, gs], nw_ref[:, gs]).astype(y_ref.dtype)


def _ssd_sample(proj, conv_state, state, cw, cb, dtb, aneg, dexp, nw, e_mat):
    bsz, width = proj.shape
    d_inner = dexp.shape[1]
    conv_w = cw.shape[1]
    bb = SAMPLE_BB
    const = lambda a: pl.BlockSpec(a.shape, lambda i: (0, 0))
    return pl.pallas_call(
        _ssd_step_kernel,
        grid=(bsz // bb,),
        in_specs=[
            pl.BlockSpec((bb, width), lambda i: (i, 0)),
            pl.BlockSpec((CONV_K - 1, bb, conv_w), lambda i: (0, i, 0)),
            pl.BlockSpec((bb, d_inner, N_M), lambda i: (i, 0, 0)),
        ] + [const(a) for a in (cw, cb, dtb, aneg, dexp, nw, e_mat)],
        out_specs=[
            pl.BlockSpec((bb, d_inner), lambda i: (i, 0)),
            pl.BlockSpec((CONV_K - 1, bb, conv_w), lambda i: (0, i, 0)),
            pl.BlockSpec((bb, d_inner, N_M), lambda i: (i, 0, 0)),
        ],
        out_shape=[
            jax.ShapeDtypeStruct((bsz, d_inner), bf16),
            jax.ShapeDtypeStruct((CONV_K - 1, bsz, conv_w), f32),
            jax.ShapeDtypeStruct((bsz, d_inner, N_M), f32),
        ],
        scratch_shapes=[pltpu.VMEM((bb, d_inner), f32)],
        compiler_params=pltpu.CompilerParams(
            dimension_semantics=("parallel",), vmem_limit_bytes=VMEM_LIMIT),
        name="ssd_sample",
    )(proj, conv_state, state, cw, cb, dtb, aneg, dexp, nw, e_mat)


def _gdn_step_kernel(pr_ref, cs_ref, st_ref, cw_ref, dtb_ref, aneg_ref, nw_ref,
                     o_ref, cs_out_ref, st_out_ref):
    bb = pr_ref.shape[0]
    n_heads = st_ref.shape[1]
    conv_w = cw_ref.shape[1]
    hk = n_heads * DK
    hv = n_heads * DV
    act = _conv_step(pr_ref[:, :conv_w], cs_ref, cs_out_ref, cw_ref, None)
    beta = jax.nn.sigmoid(pr_ref[:, conv_w + hv:conv_w + hv + LANES])
    eg = jnp.exp(aneg_ref[...] * _softplus(pr_ref[:, conv_w + hv + LANES:] + dtb_ref[...]))
    ri, _ = _iota2((bb, DV))

    heads = []
    for h in range(n_heads):
        q = act[:, h * DK:(h + 1) * DK]
        k = act[:, hk + h * DK:hk + (h + 1) * DK]
        v = act[:, 2 * hk + h * DV:2 * hk + (h + 1) * DV]
        qn = q * lax.rsqrt(jnp.sum(q * q, axis=-1, keepdims=True) + EPS) * (DK ** -0.5)
        kn = k * lax.rsqrt(jnp.sum(k * k, axis=-1, keepdims=True) + EPS)
        bcol = beta[:, h:h + 1]
        ecol = eg[:, h:h + 1]
        kb = kn * bcol
        heads.append(dict(
            u=v * bcol, w=(kb * ecol).astype(bf16), qg=(qn * ecol).astype(bf16),
            qk=jnp.sum(qn.astype(bf16).astype(f32) * kn.astype(bf16).astype(f32), axis=-1, keepdims=True),
            kn=kn.astype(bf16), ecol=ecol))

    def body(b, carry):
        pick = ri == b
        outs = []
        for h in range(n_heads):
            hd = heads[h]
            s = st_ref[b, h]
            sb = s.astype(bf16)
            vn = hd["u"] - _mm(hd["w"], sb)
            o = _mm(hd["qg"], sb) + hd["qk"] * vn
            vn_b = jnp.where(pick, vn, 0.0).astype(bf16)
            gl = jnp.sum(jnp.where(pick[:, 0:1], hd["ecol"], 0.0), axis=0, keepdims=True)
            st_out_ref[b, h] = s * gl + _tn(hd["kn"], vn_b)
            outs.append(jnp.where(pick, o, carry[h]))
        return tuple(outs)

    init = tuple(jnp.zeros((bb, DV), f32) for _ in range(n_heads))
    outs = lax.fori_loop(0, bb, body, init)
    for h in range(n_heads):
        gt = pr_ref[:, conv_w + h * DV:conv_w + (h + 1) * DV]
        o_ref[:, h * DV:(h + 1) * DV] = (_rms(outs[h], nw_ref[...]) * _silu(gt)).astype(o_ref.dtype)


def _gdn_sample(proj, conv_state, state, cw, dtb, aneg, nw):
    bsz, width = proj.shape
    n_heads = state.shape[1]
    conv_w = cw.shape[1]
    hv = n_heads * DV
    bb = SAMPLE_BB
    const = lambda a: pl.BlockSpec(a.shape, lambda i: (0, 0))
    return pl.pallas_call(
        _gdn_step_kernel,
        grid=(bsz // bb,),
        in_specs=[
            pl.BlockSpec((bb, width), lambda i: (i, 0)),
            pl.BlockSpec((CONV_K - 1, bb, conv_w), lambda i: (0, i, 0)),
            pl.BlockSpec((bb, n_heads, DK, DV), lambda i: (i, 0, 0, 0)),
        ] + [const(a) for a in (cw, dtb, aneg, nw)],
        out_specs=[
            pl.BlockSpec((bb, hv), lambda i: (i, 0)),
            pl.BlockSpec((CONV_K - 1, bb, conv_w), lambda i: (0, i, 0)),
            pl.BlockSpec((bb, n_heads, DK, DV), lambda i: (i, 0, 0, 0)),
        ],
        out_shape=[
            jax.ShapeDtypeStruct((bsz, hv), bf16),
            jax.ShapeDtypeStruct((CONV_K - 1, bsz, conv_w), f32),
            jax.ShapeDtypeStruct((bsz, n_heads, DK, DV), f32),
        ],
        compiler_params=pltpu.CompilerParams(
            dimension_semantics=("parallel",), vmem_limit_bytes=VMEM_LIMIT),
        name="gdn_sample",
    )(proj, conv_state, state, cw, dtb, aneg, nw)


def _tail_kernel(x_ref, y_ref, o_ref, p_ref, nmix_ref, wmg_ref, wbs_ref, wbg_ref, wout_ref, nf_ref, wfi_ref,
                 wfo_ref, npl_ref, wpg_ref, wpp_ref, nfin_ref, out_ref, *, ff_chunks, final_norm):
    d = x_ref.shape[1]
    d_ff = wfo_ref.shape[0]
    x = x_ref[...]
    mg = _mm(_rms(x, nmix_ref[...]).astype(bf16), wmg_ref[...])
    mix = (jax.nn.sigmoid(mg[:, :d]) * _mm(y_ref[...], wbs_ref[...])
           + jax.nn.sigmoid(mg[:, d:]) * _mm(o_ref[...], wbg_ref[...]))
    x = x + _mm(mix.astype(bf16), wout_ref[...])
    h = _rms(x, nf_ref[...]).astype(bf16)
    fc = d_ff // ff_chunks
    for c in range(ff_chunks):
        gt = _mm(h, wfi_ref[:, c * fc:(c + 1) * fc])
        up = _mm(h, wfi_ref[:, d_ff + c * fc:d_ff + (c + 1) * fc])
        x = x + _mm((_silu(gt) * up).astype(bf16), wfo_ref[c * fc:(c + 1) * fc, :])
    pe = _mm(p_ref[...].astype(bf16), wpp_ref[...])
    x = x + pe * jax.nn.sigmoid(_mm(_rms(x, npl_ref[...]).astype(bf16), wpg_ref[...]))
    if final_norm:
        x = _rms(x, nfin_ref[...])
    out_ref[...] = x


def _tail(x, y, o, p, wts, tm, final_norm):
    t, d = x.shape
    tok = lambda w: pl.BlockSpec((tm, w), lambda i: (i, 0))
    res = lambda a: pl.BlockSpec(a.shape, lambda i: (0, 0), pipeline_mode=pl.Buffered(1))
    names = ("nmix", "wmg", "wbs", "wbg", "wout", "nf", "wfi", "wfo", "npl", "wpg", "wpp", "nfin")
    return pl.pallas_call(
        functools.partial(_tail_kernel, ff_chunks=2, final_norm=final_norm),
        grid=(t // tm,),
        in_specs=[tok(d), tok(y.shape[1]), tok(o.shape[1]), tok(p.shape[1])] + [res(wts[n]) for n in names],
        out_specs=tok(d),
        out_shape=jax.ShapeDtypeStruct((t, d), f32),
        compiler_params=pltpu.CompilerParams(
            dimension_semantics=("parallel",), vmem_limit_bytes=VMEM_LIMIT),
        name="tail",
    )(x, y, o, p, *[wts[n] for n in names])


def _pad_lanes(v, width=LANES):
    return jnp.pad(v, ((0, 0), (0, width - v.shape[1])))


def _layer_weights(i, d_model, norm_mix, w_in, ssm_conv_w, ssm_conv_b, ssm_dt_bias, ssm_a_log, ssm_d, ssm_norm,
                   gdn_conv_w, gdn_dt_bias, gdn_a_log, gdn_norm, w_branch_ssm, w_branch_gdn, w_out,
                   norm_ffn, w_ffn_in, w_ffn_out, norm_pl, w_pl_gate, w_pl_proj, norm_final):
    h_m = ssm_dt_bias.shape[1]
    h_g = gdn_dt_bias.shape[1]
    d_inner = h_m * P_M
    conv_m = ssm_conv_w.shape[2]
    conv_g = gdn_conv_w.shape[2]
    sizes = (d_inner, conv_m, h_m, conv_g, h_g * DV, h_g, h_g, 2 * d_model)
    starts = [0]
    for s in sizes:
        starts.append(starts[-1] + s)
    seg = lambda j: w_in[i][:, starts[j]:starts[j + 1]]
    z, xbc, dtw, qkv, gate, bw, aw, mg = (seg(j) for j in range(8))
    w_ssd = jnp.concatenate([xbc, z, _pad_lanes(dtw)], axis=1).astype(bf16)
    w_gdn = jnp.concatenate([qkv, gate, _pad_lanes(bw), _pad_lanes(aw)], axis=1).astype(bf16)
    head_of = jnp.arange(d_inner, dtype=jnp.int32) // P_M
    e_mat = (jnp.arange(LANES, dtype=jnp.int32)[:, None] == head_of[None, :]).astype(bf16)
    e_mat = jnp.concatenate([e_mat] * 3, axis=0)
    row = lambda v: v.reshape(1, -1).astype(f32)
    return dict(
        norm_mix=row(norm_mix[i]), w_ssd=w_ssd, w_gdn=w_gdn,
        ssm=dict(cw=ssm_conv_w[i], cb=row(ssm_conv_b[i]), dtb=_pad_lanes(row(ssm_dt_bias[i])),
                 aneg=_pad_lanes(-jnp.exp(row(ssm_a_log[i]))), dexp=row(jnp.repeat(ssm_d[i], P_M)),
                 nw=row(ssm_norm[i]), e_mat=e_mat),
        gdn=dict(cw=gdn_conv_w[i], dtb=_pad_lanes(row(gdn_dt_bias[i])),
                 aneg=_pad_lanes(-jnp.exp(row(gdn_a_log[i]))), nw=row(gdn_norm[i])),
        tail=dict(nmix=row(norm_mix[i]), wmg=mg.astype(bf16),
                  wbs=w_branch_ssm[i].astype(bf16), wbg=w_branch_gdn[i].astype(bf16), wout=w_out[i].astype(bf16),
                  nf=row(norm_ffn[i]), wfi=w_ffn_in[i].astype(bf16), wfo=w_ffn_out[i].astype(bf16),
                  npl=row(norm_pl[i]), wpg=w_pl_gate[i].astype(bf16), wpp=w_pl_proj[i].astype(bf16),
                  nfin=row(norm_final)),
    )


def _pick_tile(n, candidates):
    for c in candidates:
        if n % c == 0:
            return c
    raise ValueError(f"no tile for {n}")


def kernel(x_prompt, x_sample, p_prompt, p_sample, state_ssm, state_ssm_conv, state_gdn, state_gdn_conv,
           norm_mix, w_in, ssm_conv_w, ssm_conv_b, ssm_dt_bias, ssm_a_log, ssm_d, ssm_norm,
           gdn_conv_w, gdn_dt_bias, gdn_a_log, gdn_norm, w_branch_ssm, w_branch_gdn, w_out,
           norm_ffn, w_ffn_in, w_ffn_out, norm_pl, w_pl_gate, w_pl_proj, norm_final):
    depth = p_prompt.shape[0]
    bp, seq, d_model = x_prompt.shape
    bs, dec_seq, _ = x_sample.shape
    assert dec_seq == 1 and seq % STEP_TOKENS == 0 and bs % SAMPLE_BB == 0
    h_m, h_g = ssm_dt_bias.shape[1], gdn_dt_bias.shape[1]
    d_inner = h_m * P_M
    tail_rows = slice(SUBLANES - (CONV_K - 1), SUBLANES)

    xp = x_prompt.reshape(bp * seq, d_model)
    xs = x_sample.reshape(bs, d_model)
    new_p = ([], [], [], [])
    new_s = ([], [], [], [])
    for i in range(depth):
        lw = _layer_weights(i, d_model, norm_mix, w_in, ssm_conv_w, ssm_conv_b, ssm_dt_bias, ssm_a_log, ssm_d,
                            ssm_norm, gdn_conv_w, gdn_dt_bias, gdn_a_log, gdn_norm, w_branch_ssm, w_branch_gdn,
                            w_out, norm_ffn, w_ffn_in, w_ffn_out, norm_pl, w_pl_gate, w_pl_proj, norm_final)
        last = i == depth - 1

        tp = bp * seq
        y, st_ssm, cs_ssm = _ssd_prompt(xp, lw["norm_mix"], lw["w_ssd"], bp, seq, **lw["ssm"])
        o, st_gdn, cs_gdn = _gdn_prompt(xp, lw["norm_mix"], lw["w_gdn"], bp, seq, n_heads=h_g, **lw["gdn"])
        xp = _tail(xp, y, o, p_prompt[i].reshape(tp, -1), lw["tail"], _pick_tile(tp, (256, 128)), last)
        new_p[0].append(st_ssm.reshape(bp, h_m, P_M, N_M))
        new_p[1].append(cs_ssm[:, tail_rows, :])
        new_p[2].append(st_gdn)
        new_p[3].append(cs_gdn[:, tail_rows, :])

        ys, cs_s, ss_s = _ssd_sample(_inproj(xs, lw["norm_mix"], lw["w_ssd"]),
                                     jnp.swapaxes(state_ssm_conv[i], 0, 1),
                                     state_ssm[i].reshape(bs, d_inner, N_M), **lw["ssm"])
        os_, cs_g, ss_g = _gdn_sample(_inproj(xs, lw["norm_mix"], lw["w_gdn"]),
                                      jnp.swapaxes(state_gdn_conv[i], 0, 1), state_gdn[i], **lw["gdn"])
        xs = _tail(xs, ys, os_, p_sample[i].reshape(bs, -1), lw["tail"],
                   _pick_tile(bs, (128, 64, 32, 16, 8)), last)
        new_s[0].append(ss_s.reshape(bs, h_m, P_M, N_M))
        new_s[1].append(jnp.swapaxes(cs_s, 0, 1))
        new_s[2].append(ss_g)
        new_s[3].append(jnp.swapaxes(cs_g, 0, 1))

    stack = lambda lst: jnp.stack(lst)
    return (xp.reshape(bp, seq, d_model), xs.reshape(bs, 1, d_model),
            stack(new_p[0]), stack(new_p[1]), stack(new_p[2]), stack(new_p[3]),
            stack(new_s[0]), stack(new_s[1]), stack(new_s[2]), stack(new_s[3]))
```

```python
import functools

import jax
import jax.numpy as jnp
from jax import lax
from jax.experimental import pallas as pl
from jax.experimental.pallas import tpu as pltpu

f32 = jnp.float32
bf16 = jnp.bfloat16

EPS = 1e-6
CONV_K = 4
LANES = 128
SUBLANES = 8
VMEM_LIMIT = 56 * 1024 * 1024

P_M = 64
N_M = 128
G_M = 4
DK = 128
DV = 128
SSD_BLOCK = 128
GDN_CHUNK = 64
GDN_BLOCK = 128
GDN_BASE = 16
STEP_TOKENS = 256
CONV_COLS = 512
SAMPLE_BB = 8


def _nt(a, b):
    return lax.dot_general(a, b, (((1,), (1,)), ((), ())), preferred_element_type=f32)


def _tn(a, b):
    return lax.dot_general(a, b, (((0,), (0,)), ((), ())), preferred_element_type=f32)


def _mm(a, b):
    return jnp.dot(a, b, preferred_element_type=f32)


def _split3(x):
    hi = x.astype(bf16)
    r1 = x - hi.astype(f32)
    mid = r1.astype(bf16)
    lo = (r1 - mid.astype(f32)).astype(bf16)
    return hi, mid, lo


def _mm_sel_rhs(x, sel3):
    return _mm(jnp.concatenate(_split3(x), axis=1), sel3)


def _mm_sel_lhs(sel, x):
    return _mm(jnp.concatenate([sel] * 3, axis=1), jnp.concatenate(_split3(x), axis=0))


def _silu(x):
    return x * jax.nn.sigmoid(x)


def _softplus(x):
    return jnp.maximum(x, 0.0) + jnp.log1p(jnp.exp(-jnp.abs(x)))


def _rms(x, g):
    return x * lax.rsqrt(jnp.mean(x * x, axis=-1, keepdims=True) + EPS) * g


def _iota2(shape):
    return lax.broadcasted_iota(jnp.int32, shape, 0), lax.broadcasted_iota(jnp.int32, shape, 1)


def _inproj_kernel(x_ref, g_ref, w_ref, o_ref):
    o_ref[...] = _mm(_rms(x_ref[...], g_ref[...]).astype(bf16), w_ref[...])


def _inproj(x, g, w):
    t, d = x.shape
    n = w.shape[1]
    full = lambda shape: pl.BlockSpec(shape, lambda i: (0, 0))
    return pl.pallas_call(
        _inproj_kernel,
        grid=(1,),
        in_specs=[full((t, d)), full((1, d)), full((d, n))],
        out_specs=full((t, n)),
        out_shape=jax.ShapeDtypeStruct((t, n), f32),
        compiler_params=pltpu.CompilerParams(dimension_semantics=("arbitrary",), vmem_limit_bytes=VMEM_LIMIT),
        name="inproj",
    )(x, g, w)


def _project_conv(h, w_ref, ext_ref, act_ref, cw_ref, cb_ref, rows, width):
    for c0 in range(0, width, CONV_COLS):
        sl = slice(c0, c0 + CONV_COLS)
        u = _mm(h, w_ref[:, sl])
        ext_ref[SUBLANES:SUBLANES + rows, sl] = u
        cw = cw_ref[:, sl]
        conv = u * cw[3:4]
        for j in range(CONV_K - 1):
            off = SUBLANES - (CONV_K - 1) + j
            conv = conv + ext_ref[off:off + rows, sl] * cw[j:j + 1]
        if cb_ref is not None:
            conv = conv + cb_ref[:, sl]
        act_ref[:, sl] = _silu(conv)
        ext_ref[0:SUBLANES, sl] = ext_ref[rows:rows + SUBLANES, sl]


def _ssd_kernel(x_ref, g_ref, w_ref, cw_ref, cb_ref, dtb_ref, aneg_ref, dexp_ref, nw_ref, e_ref,
                y_ref, st_out_ref, cs_out_ref, ext_ref, act_ref, z_ref, st_ref, yd_ref):
    i = pl.program_id(1)
    rows = x_ref.shape[0]
    c = SSD_BLOCK
    conv_w = act_ref.shape[1]
    d_inner = y_ref.shape[1]
    gw = d_inner // G_M
    hg = gw // P_M

    @pl.when(i == 0)
    def _():
        ext_ref[0:SUBLANES, :] = jnp.zeros((SUBLANES, conv_w), f32)
        st_ref[...] = jnp.zeros(st_ref.shape, f32)

    h = _rms(x_ref[...], g_ref[...]).astype(bf16)
    _project_conv(h, w_ref, ext_ref, act_ref, cw_ref, cb_ref, rows, conv_w)
    dt_raw = _mm(h, w_ref[:, conv_w + d_inner:])
    for g in range(G_M):
        z_ref[:, g * gw:(g + 1) * gw] = _mm(h, w_ref[:, conv_w + g * gw:conv_w + (g + 1) * gw])

    r, cc = _iota2((c, c))
    lower = r >= cc
    lower_b = jnp.where(lower, 1.0, 0.0).astype(bf16)
    upper3 = jnp.concatenate([jnp.where(r <= cc, 1.0, 0.0).astype(bf16)] * 3, axis=0)

    for sub in range(rows // c):
        rs = slice(sub * c, (sub + 1) * c)
        dt = _softplus(dt_raw[rs] + dtb_ref[...])
        da = dt * aneg_ref[...]
        acum = _mm_sel_lhs(lower_b, da)
        acum_t = _mm_sel_rhs(da.T, upper3)
        alast = acum[c - 1:c, :]
        dout = jnp.exp(alast - acum)
        scal = jnp.concatenate(
            [dt, dt * dout, jnp.exp(acum), jnp.broadcast_to(jnp.exp(alast), (SUBLANES, LANES))], axis=0)
        for g in range(G_M):
            gs = slice(g * gw, (g + 1) * gw)
            se = _mm_sel_rhs(scal, e_ref[:, gs])
            dt_e, dd_e, ea_e, cd_e = se[0:c], se[c:2 * c], se[2 * c:3 * c], se[3 * c:3 * c + 1]
            xs = act_ref[rs, gs]
            bm = act_ref[rs, d_inner + g * N_M:d_inner + (g + 1) * N_M].astype(bf16)
            cm = act_ref[rs, d_inner + (G_M + g) * N_M:d_inner + (G_M + g + 1) * N_M].astype(bf16)
            xdt = (xs * dt_e).astype(bf16)
            cb = _nt(cm, bm)
            st = st_ref[:, gs]
            y_off = _mm(cm, st.astype(bf16)) * ea_e
            st_ref[:, gs] = st * cd_e + _tn(bm, (xs * dd_e).astype(bf16))
            for hh in range(hg):
                hd = g * hg + hh
                lmat = jnp.exp(jnp.where(lower, acum[:, hd:hd + 1] - acum_t[hd:hd + 1, :], -jnp.inf))
                yd_ref[:, hh * P_M:(hh + 1) * P_M] = _mm((cb * lmat).astype(bf16), xdt[:, hh * P_M:(hh + 1) * P_M])
            y = yd_ref[...] + y_off + xs * dexp_ref[:, gs]
            y = y * _silu(z_ref[rs, gs])
            y_ref[rs, gs] = _rms(y, nw_ref[:, gs]).astype(y_ref.dtype)

    @pl.when(i == pl.num_programs(1) - 1)
    def _():
        cs_out_ref[0] = ext_ref[0:SUBLANES, :]
        for j in range(d_inner // LANES):
            st_out_ref[0, j * LANES:(j + 1) * LANES, :] = st_ref[:, j * LANES:(j + 1) * LANES].T


def _ssd_prompt(x, g, w, bsz, seq, cw, cb, dtb, aneg, dexp, nw, e_mat):
    d_model = x.shape[1]
    d_inner = dexp.shape[1]
    conv_w = cw.shape[1]
    rows = STEP_TOKENS
    nb = seq // rows
    tok = lambda b, i: (b * nb + i, 0)
    res = lambda a: pl.BlockSpec(a.shape, lambda b, i: (0, 0), pipeline_mode=pl.Buffered(1))
    return pl.pallas_call(
        _ssd_kernel,
        grid=(bsz, nb),
        in_specs=[pl.BlockSpec((rows, d_model), tok)] + [res(a) for a in (g, w, cw, cb, dtb, aneg, dexp, nw, e_mat)],
        out_specs=[
            pl.BlockSpec((rows, d_inner), tok),
            pl.BlockSpec((1, d_inner, N_M), lambda b, i: (b, 0, 0)),
            pl.BlockSpec((1, SUBLANES, conv_w), lambda b, i: (b, 0, 0)),
        ],
        out_shape=[
            jax.ShapeDtypeStruct((bsz * seq, d_inner), bf16),
            jax.ShapeDtypeStruct((bsz, d_inner, N_M), f32),
            jax.ShapeDtypeStruct((bsz, SUBLANES, conv_w), f32),
        ],
        scratch_shapes=[
            pltpu.VMEM((rows + SUBLANES, conv_w), f32),
            pltpu.VMEM((rows, conv_w), f32),
            pltpu.VMEM((rows, d_inner), f32),
            pltpu.VMEM((N_M, d_inner), f32),
            pltpu.VMEM((SSD_BLOCK, d_inner // G_M), f32),
        ],
        compiler_params=pltpu.CompilerParams(
            dimension_semantics=("parallel", "arbitrary"), vmem_limit_bytes=VMEM_LIMIT),
        name="ssd_prompt",
    )(x, g, w, cw, cb, dtb, aneg, dexp, nw, e_mat)


def _gdn_kernel(x_ref, g_ref, w_ref, cw_ref, dtb_ref, aneg_ref, nw_ref,
                o_ref, st_out_ref, cs_out_ref, ext_ref, act_ref, gate_ref, st_ref):
    i = pl.program_id(1)
    rows = x_ref.shape[0]
    blk = GDN_BLOCK
    ck = GDN_CHUNK
    conv_w = act_ref.shape[1]
    n_heads = st_ref.shape[0]
    hk = n_heads * DK
    hv = n_heads * DV
    per_blk = blk // ck
    heads = range(n_heads)

    @pl.when(i == 0)
    def _():
        ext_ref[0:SUBLANES, :] = jnp.zeros((SUBLANES, conv_w), f32)
        st_ref[...] = jnp.zeros(st_ref.shape, f32)

    h_in = _rms(x_ref[...], g_ref[...]).astype(bf16)
    _project_conv(h_in, w_ref, ext_ref, act_ref, cw_ref, None, rows, conv_w)
    ba = _mm(h_in, w_ref[:, conv_w + hv:])
    gate_ref[...] = _mm(h_in, w_ref[:, conv_w:conv_w + hv])

    r, cc = _iota2((blk, blk))
    same = (r // ck) == (cc // ck)
    lower = same & (r >= cc)
    strict = same & (r > cc)
    lower_b = jnp.where(lower, 1.0, 0.0).astype(bf16)
    upper3 = jnp.concatenate([jnp.where(same & (r <= cc), 1.0, 0.0).astype(bf16)] * 3, axis=0)
    rcol = r[:, 0:1]
    eye = jnp.where(r == cc, 1.0, 0.0)
    blk_masks = []
    size = GDN_BASE
    while size <= ck:
        blk_masks.append((r // size) == (cc // size))
        size *= 2

    xs, qks, qgs, kgs, gls = [], [], [], [], []
    for sub in range(rows // blk):
        rs = slice(sub * blk, (sub + 1) * blk)
        beta = jax.nn.sigmoid(ba[rs, :LANES])
        glog = aneg_ref[...] * _softplus(ba[rs, LANES:] + dtb_ref[...])
        gcum = _mm_sel_lhs(lower_b, glog)
        gcum_t = _mm_sel_rhs(glog.T, upper3)
        nmats, xsub = [], []
        for h in heads:
            q = act_ref[rs, h * DK:(h + 1) * DK]
            k = act_ref[rs, hk + h * DK:hk + (h + 1) * DK]
            v = act_ref[rs, 2 * hk + h * DV:2 * hk + (h + 1) * DV]
            qn = q * lax.rsqrt(jnp.sum(q * q, axis=-1, keepdims=True) + EPS) * (DK ** -0.5)
            kn = k * lax.rsqrt(jnp.sum(k * k, axis=-1, keepdims=True) + EPS)
            bcol = beta[:, h:h + 1]
            gcol = gcum[:, h:h + 1]
            eg = jnp.exp(gcol)
            dm = jnp.exp(jnp.where(lower, gcol - gcum_t[h:h + 1, :], -jnp.inf))
            knb = kn.astype(bf16)
            kb = kn * bcol
            nmats.append(jnp.where(strict, -(_nt(kb.astype(bf16), knb) * dm), 0.0))
            xsub.append(jnp.concatenate([v * bcol, kb * eg], axis=-1))
            qks.append(jnp.where(lower, _nt(qn.astype(bf16), knb) * dm, 0.0).astype(bf16))
            qgs.append((qn * eg).astype(bf16))
            glast = [gcol[(j + 1) * ck - 1:(j + 1) * ck] for j in range(per_blk)]
            glast_row = glast[per_blk - 1]
            for j in range(per_blk - 2, -1, -1):
                glast_row = jnp.where(rcol < (j + 1) * ck, glast[j], glast_row)
            kgs.append((kn * jnp.exp(glast_row - gcol)).astype(bf16))
            gls.append([jnp.exp(gl) for gl in glast])
        qs = [jnp.where(blk_masks[0], n, 0.0) for n in nmats]
        minv = [eye + q for q in qs]
        span = 2
        while span < GDN_BASE:
            qbs = [q.astype(bf16) for q in qs]
            qs = [_mm(qb, qb) for qb in qbs]
            minv = [m + _mm(m.astype(bf16), q.astype(bf16)) for m, q in zip(minv, qs)]
            span *= 2
        for lvl in range(1, len(blk_masks)):
            off = blk_masks[lvl] & jnp.logical_not(blk_masks[lvl - 1])
            mbs = [m.astype(bf16) for m in minv]
            ems = [_mm(jnp.where(off, n, 0.0).astype(bf16), mb) for n, mb in zip(nmats, mbs)]
            minv = [m + _mm(mb, em.astype(bf16)) for m, mb, em in zip(minv, mbs, ems)]
        xs.extend(_mm(m.astype(bf16), x.astype(bf16)) for m, x in zip(minv, xsub))

    ss = [st_ref[h] for h in heads]
    zeros = jnp.zeros((ck, DV), bf16)
    for jc in range(rows // ck):
        sub, j = divmod(jc, per_blk)
        rl = slice(j * ck, (j + 1) * ck)
        ro = slice(jc * ck, (jc + 1) * ck)
        at = lambda lst, h: lst[sub * n_heads + h]
        sbs = [s.astype(bf16) for s in ss]
        vns = [at(xs, h)[rl, :DV] - _mm(at(xs, h)[rl, DV:].astype(bf16), sbs[h]) for h in heads]
        vfull = [jnp.concatenate([zeros] * j + [vns[h].astype(bf16)] + [zeros] * (per_blk - 1 - j), axis=0)
                 for h in heads]
        outs = [_mm(at(qgs, h)[rl], sbs[h]) + _mm(at(qks, h)[rl, :], vfull[h]) for h in heads]
        ss = [ss[h] * at(gls, h)[j] + _tn(at(kgs, h)[rl], vns[h].astype(bf16)) for h in heads]
        for h in heads:
            gt = gate_ref[ro, h * DV:(h + 1) * DV]
            o_ref[ro, h * DV:(h + 1) * DV] = (_rms(outs[h], nw_ref[...]) * _silu(gt)).astype(o_ref.dtype)
    for h in heads:
        st_ref[h] = ss[h]

    @pl.when(i == pl.num_programs(1) - 1)
    def _():
        cs_out_ref[0] = ext_ref[0:SUBLANES, :]
        st_out_ref[0] = st_ref[...]


def _gdn_prompt(x, g, w, bsz, seq, cw, dtb, aneg, nw, n_heads):
    d_model = x.shape[1]
    conv_w = cw.shape[1]
    hv = n_heads * DV
    rows = STEP_TOKENS
    nb = seq // rows
    tok = lambda b, i: (b * nb + i, 0)
    res = lambda a: pl.BlockSpec(a.shape, lambda b, i: (0, 0), pipeline_mode=pl.Buffered(1))
    return pl.pallas_call(
        _gdn_kernel,
        grid=(bsz, nb),
        in_specs=[pl.BlockSpec((rows, d_model), tok)] + [res(a) for a in (g, w, cw, dtb, aneg, nw)],
        out_specs=[
            pl.BlockSpec((rows, hv), tok),
            pl.BlockSpec((1, n_heads, DK, DV), lambda b, i: (b, 0, 0, 0)),
            pl.BlockSpec((1, SUBLANES, conv_w), lambda b, i: (b, 0, 0)),
        ],
        out_shape=[
            jax.ShapeDtypeStruct((bsz * seq, hv), bf16),
            jax.ShapeDtypeStruct((bsz, n_heads, DK, DV), f32),
            jax.ShapeDtypeStruct((bsz, SUBLANES, conv_w), f32),
        ],
        scratch_shapes=[
            pltpu.VMEM((rows + SUBLANES, conv_w), f32),
            pltpu.VMEM((rows, conv_w), f32),
            pltpu.VMEM((rows, hv), f32),
            pltpu.VMEM((n_heads, DK, DV), f32),
        ],
        compiler_params=pltpu.CompilerParams(
            dimension_semantics=("parallel", "arbitrary"), vmem_limit_bytes=VMEM_LIMIT),
        name="gdn_prompt",
    )(x, g, w, cw, dtb, aneg, nw)


def _conv_step(u, cs_ref, cs_out_ref, cw_ref, cb_ref):
    cw = cw_ref[...]
    conv = u * cw[3:4]
    for j in range(CONV_K - 1):
        conv = conv + cs_ref[j] * cw[j:j + 1]
    if cb_ref is not None:
        conv = conv + cb_ref[...]
    for j in range(CONV_K - 2):
        cs_out_ref[j] = cs_ref[j + 1]
    cs_out_ref[CONV_K - 2] = u
    return _silu(conv)


def _ssd_step_kernel(pr_ref, cs_ref, st_ref, cw_ref, cb_ref, dtb_ref, aneg_ref, dexp_ref,
                     nw_ref, e_ref, y_ref, cs_out_ref, st_out_ref, yacc_ref):
    bb = pr_ref.shape[0]
    d_inner = y_ref.shape[1]
    conv_w = cw_ref.shape[1]
    gw = d_inner // G_M
    act = _conv_step(pr_ref[:, :conv_w], cs_ref, cs_out_ref, cw_ref, cb_ref)
    z = pr_ref[:, conv_w:conv_w + d_inner]
    dt = _softplus(pr_ref[:, conv_w + d_inner:] + dtb_ref[...])
    dec = jnp.exp(dt * aneg_ref[...])
    se = _mm_sel_rhs(jnp.concatenate([dt, dec], axis=0), e_ref[...])
    xs = act[:, :d_inner]
    xdt = xs * se[0:bb]
    dec_e = se[bb:2 * bb]
    d3 = jnp.concatenate(_split3(dec_e), axis=0)
    r3, c3 = _iota2((3 * bb, bb * N_M))
    seq3 = c3 // N_M
    onehot3 = jnp.where((r3 == seq3) | (r3 == seq3 + bb) | (r3 == seq3 + 2 * bb), 1.0, 0.0).astype(bf16)
    rb, cb_i = _iota2((bb, bb * N_M))
    own = rb == cb_i // N_M
    rowid = rb[:, 0:1]

    for g in range(G_M):
        gs = slice(g * gw, (g + 1) * gw)
        bm = act[:, d_inner + g * N_M:d_inner + (g + 1) * N_M]
        cm = act[:, d_inner + (G_M + g) * N_M:d_inner + (G_M + g + 1) * N_M].astype(bf16)
        bdiag = jnp.where(own, jnp.concatenate([bm] * bb, axis=1), 0.0).astype(bf16)
        dcol = _tn(d3[:, gs], onehot3)
        outer = _tn(xdt[:, gs].astype(bf16), bdiag)
        yg = None
        for b in range(bb):
            bs = slice(b * N_M, (b + 1) * N_M)
            s_new = st_ref[b, gs, :] * dcol[:, bs] + outer[:, bs]
            st_out_ref[b, gs, :] = s_new
            yb = _nt(cm, s_new.astype(bf16))
            yg = yb if yg is None else jnp.where(rowid == b, yb, yg)
        yacc_ref[:, gs] = yg

    y = yacc_ref[...] + xs * dexp_ref[...]
    y = y * _silu(z)
    for g in range(G_M):
        gs = slice(g * gw, (g + 1) * gw)
        y_ref[:, gs] = _rms(y[:, gs], nw_ref[:, gs]).astype(y_ref.dtype)


def _ssd_sample(proj, conv_state, state, cw, cb, dtb, aneg, dexp, nw, e_mat):
    bsz, width = proj.shape
    d_inner = dexp.shape[1]
    conv_w = cw.shape[1]
    bb = SAMPLE_BB
    const = lambda a: pl.BlockSpec(a.shape, lambda i: (0, 0))
    return pl.pallas_call(
        _ssd_step_kernel,
        grid=(bsz // bb,),
        in_specs=[
            pl.BlockSpec((bb, width), lambda i: (i, 0)),
            pl.BlockSpec((CONV_K - 1, bb, conv_w), lambda i: (0, i, 0)),
            pl.BlockSpec((bb, d_inner, N_M), lambda i: (i, 0, 0)),
        ] + [const(a) for a in (cw, cb, dtb, aneg, dexp, nw, e_mat)],
        out_specs=[
            pl.BlockSpec((bb, d_inner), lambda i: (i, 0)),
            pl.BlockSpec((CONV_K - 1, bb, conv_w), lambda i: (0, i, 0)),
            pl.BlockSpec((bb, d_inner, N_M), lambda i: (i, 0, 0)),
        ],
        out_shape=[
            jax.ShapeDtypeStruct((bsz, d_inner), bf16),
            jax.ShapeDtypeStruct((CONV_K - 1, bsz, conv_w), f32),
            jax.ShapeDtypeStruct((bsz, d_inner, N_M), f32),
        ],
        scratch_shapes=[pltpu.VMEM((bb, d_inner), f32)],
        compiler_params=pltpu.CompilerParams(
            dimension_semantics=("parallel",), vmem_limit_bytes=VMEM_LIMIT),
        name="ssd_sample",
    )(proj, conv_state, state, cw, cb, dtb, aneg, dexp, nw, e_mat)


def _gdn_step_kernel(pr_ref, cs_ref, st_ref, cw_ref, dtb_ref, aneg_ref, nw_ref,
                     o_ref, cs_out_ref, st_out_ref):
    bb = pr_ref.shape[0]
    n_heads = st_ref.shape[1]
    conv_w = cw_ref.shape[1]
    hk = n_heads * DK
    hv = n_heads * DV
    act = _conv_step(pr_ref[:, :conv_w], cs_ref, cs_out_ref, cw_ref, None)
    beta = jax.nn.sigmoid(pr_ref[:, conv_w + hv:conv_w + hv + LANES])
    eg = jnp.exp(aneg_ref[...] * _softplus(pr_ref[:, conv_w + hv + LANES:] + dtb_ref[...]))
    rb, cb_i = _iota2((bb, bb * DV))
    own = rb == cb_i // DV
    rowid = rb[:, 0:1]

    for h in range(n_heads):
        q = act[:, h * DK:(h + 1) * DK]
        k = act[:, hk + h * DK:hk + (h + 1) * DK]
        v = act[:, 2 * hk + h * DV:2 * hk + (h + 1) * DV]
        qn = q * lax.rsqrt(jnp.sum(q * q, axis=-1, keepdims=True) + EPS) * (DK ** -0.5)
        kn = k * lax.rsqrt(jnp.sum(k * k, axis=-1, keepdims=True) + EPS)
        bcol = beta[:, h:h + 1]
        ecol = eg[:, h:h + 1]
        kb = kn * bcol
        u = v * bcol
        qk = jnp.sum(qn.astype(bf16).astype(f32) * kn.astype(bf16).astype(f32), axis=-1, keepdims=True)
        wq = jnp.concatenate([(kb * ecol).astype(bf16), (qn * ecol).astype(bf16)], axis=0)
        vn = o = None
        for b in range(bb):
            ws = _mm(wq, st_ref[b, h].astype(bf16))
            vn_b = u - ws[:bb]
            o_b = ws[bb:] + qk * vn_b
            vn = vn_b if vn is None else jnp.where(rowid == b, vn_b, vn)
            o = o_b if o is None else jnp.where(rowid == b, o_b, o)
        vdiag = jnp.where(own, jnp.concatenate([vn] * bb, axis=1), 0.0).astype(bf16)
        outer = _tn(kn.astype(bf16), vdiag)
        for b in range(bb):
            st_out_ref[b, h] = st_ref[b, h] * ecol[b:b + 1, :] + outer[:, b * DV:(b + 1) * DV]
        gt = pr_ref[:, conv_w + h * DV:conv_w + (h + 1) * DV]
        o_ref[:, h * DV:(h + 1) * DV] = (_rms(o, nw_ref[...]) * _silu(gt)).astype(o_ref.dtype)


def _gdn_sample(proj, conv_state, state, cw, dtb, aneg, nw):
    bsz, width = proj.shape
    n_heads = state.shape[1]
    conv_w = cw.shape[1]
    hv = n_heads * DV
    bb = SAMPLE_BB
    const = lambda a: pl.BlockSpec(a.shape, lambda i: (0, 0))
    return pl.pallas_call(
        _gdn_step_kernel,
        grid=(bsz // bb,),
        in_specs=[
            pl.BlockSpec((bb, width), lambda i: (i, 0)),
            pl.BlockSpec((CONV_K - 1, bb, conv_w), lambda i: (0, i, 0)),
            pl.BlockSpec((bb, n_heads, DK, DV), lambda i: (i, 0, 0, 0)),
        ] + [const(a) for a in (cw, dtb, aneg, nw)],
        out_specs=[
            pl.BlockSpec((bb, hv), lambda i: (i, 0)),
            pl.BlockSpec((CONV_K - 1, bb, conv_w), lambda i: (0, i, 0)),
            pl.BlockSpec((bb, n_heads, DK, DV), lambda i: (i, 0, 0, 0)),
        ],
        out_shape=[
            jax.ShapeDtypeStruct((bsz, hv), bf16),
            jax.ShapeDtypeStruct((CONV_K - 1, bsz, conv_w), f32),
            jax.ShapeDtypeStruct((bsz, n_heads, DK, DV), f32),
        ],
        compiler_params=pltpu.CompilerParams(
            dimension_semantics=("parallel",), vmem_limit_bytes=VMEM_LIMIT),
        name="gdn_sample",
    )(proj, conv_state, state, cw, dtb, aneg, nw)


def _tail_kernel(x_ref, y_ref, o_ref, p_ref, nmix_ref, wmg_ref, wbs_ref, wbg_ref, wout_ref, nf_ref, wfi_ref,
                 wfo_ref, npl_ref, wpg_ref, wpp_ref, nfin_ref, out_ref, *, ff_chunks, final_norm):
    d = x_ref.shape[1]
    d_ff = wfo_ref.shape[0]
    x = x_ref[...]
    mg = _mm(_rms(x, nmix_ref[...]).astype(bf16), wmg_ref[...])
    mix = (jax.nn.sigmoid(mg[:, :d]) * _mm(y_ref[...], wbs_ref[...])
           + jax.nn.sigmoid(mg[:, d:]) * _mm(o_ref[...], wbg_ref[...]))
    x = x + _mm(mix.astype(bf16), wout_ref[...])
    h = _rms(x, nf_ref[...]).astype(bf16)
    fc = d_ff // ff_chunks
    for c in range(ff_chunks):
        gt = _mm(h, wfi_ref[:, c * fc:(c + 1) * fc])
        up = _mm(h, wfi_ref[:, d_ff + c * fc:d_ff + (c + 1) * fc])
        x = x + _mm((_silu(gt) * up).astype(bf16), wfo_ref[c * fc:(c + 1) * fc, :])
    pe = _mm(p_ref[...].astype(bf16), wpp_ref[...])
    x = x + pe * jax.nn.sigmoid(_mm(_rms(x, npl_ref[...]).astype(bf16), wpg_ref[...]))
    if final_norm:
        x = _rms(x, nfin_ref[...])
    out_ref[...] = x


def _tail(x, y, o, p, wts, tm, final_norm):
    t, d = x.shape
    tok = lambda w: pl.BlockSpec((tm, w), lambda i: (i, 0))
    res = lambda a: pl.BlockSpec(a.shape, lambda i: (0, 0), pipeline_mode=pl.Buffered(1))
    names = ("nmix", "wmg", "wbs", "wbg", "wout", "nf", "wfi", "wfo", "npl", "wpg", "wpp", "nfin")
    return pl.pallas_call(
        functools.partial(_tail_kernel, ff_chunks=2, final_norm=final_norm),
        grid=(t // tm,),
        in_specs=[tok(d), tok(y.shape[1]), tok(o.shape[1]), tok(p.shape[1])] + [res(wts[n]) for n in names],
        out_specs=tok(d),
        out_shape=jax.ShapeDtypeStruct((t, d), f32),
        compiler_params=pltpu.CompilerParams(
            dimension_semantics=("parallel",), vmem_limit_bytes=VMEM_LIMIT),
        name="tail",
    )(x, y, o, p, *[wts[n] for n in names])


def _pad_lanes(v, width=LANES):
    return jnp.pad(v, ((0, 0), (0, width - v.shape[1])))


def _layer_weights(i, d_model, norm_mix, w_in, ssm_conv_w, ssm_conv_b, ssm_dt_bias, ssm_a_log, ssm_d, ssm_norm,
                   gdn_conv_w, gdn_dt_bias, gdn_a_log, gdn_norm, w_branch_ssm, w_branch_gdn, w_out,
                   norm_ffn, w_ffn_in, w_ffn_out, norm_pl, w_pl_gate, w_pl_proj, norm_final):
    h_m = ssm_dt_bias.shape[1]
    h_g = gdn_dt_bias.shape[1]
    d_inner = h_m * P_M
    conv_m = ssm_conv_w.shape[2]
    conv_g = gdn_conv_w.shape[2]
    sizes = (d_inner, conv_m, h_m, conv_g, h_g * DV, h_g, h_g, 2 * d_model)
    starts = [0]
    for s in sizes:
        starts.append(starts[-1] + s)
    seg = lambda j: w_in[i][:, starts[j]:starts[j + 1]]
    z, xbc, dtw, qkv, gate, bw, aw, mg = (seg(j) for j in range(8))
    w_ssd = jnp.concatenate([xbc, z, _pad_lanes(dtw)], axis=1).astype(bf16)
    w_gdn = jnp.concatenate([qkv, gate, _pad_lanes(bw), _pad_lanes(aw)], axis=1).astype(bf16)
    head_of = jnp.arange(d_inner, dtype=jnp.int32) // P_M
    e_mat = (jnp.arange(LANES, dtype=jnp.int32)[:, None] == head_of[None, :]).astype(bf16)
    e_mat = jnp.concatenate([e_mat] * 3, axis=0)
    row = lambda v: v.reshape(1, -1).astype(f32)
    return dict(
        norm_mix=row(norm_mix[i]), w_ssd=w_ssd, w_gdn=w_gdn,
        ssm=dict(cw=ssm_conv_w[i], cb=row(ssm_conv_b[i]), dtb=_pad_lanes(row(ssm_dt_bias[i])),
                 aneg=_pad_lanes(-jnp.exp(row(ssm_a_log[i]))), dexp=row(jnp.repeat(ssm_d[i], P_M)),
                 nw=row(ssm_norm[i]), e_mat=e_mat),
        gdn=dict(cw=gdn_conv_w[i], dtb=_pad_lanes(row(gdn_dt_bias[i])),
                 aneg=_pad_lanes(-jnp.exp(row(gdn_a_log[i]))), nw=row(gdn_norm[i])),
        tail=dict(nmix=row(norm_mix[i]), wmg=mg.astype(bf16),
                  wbs=w_branch_ssm[i].astype(bf16), wbg=w_branch_gdn[i].astype(bf16), wout=w_out[i].astype(bf16),
                  nf=row(norm_ffn[i]), wfi=w_ffn_in[i].astype(bf16), wfo=w_ffn_out[i].astype(bf16),
                  npl=row(norm_pl[i]), wpg=w_pl_gate[i].astype(bf16), wpp=w_pl_proj[i].astype(bf16),
                  nfin=row(norm_final)),
    )


def _pick_tile(n, candidates):
    for c in candidates:
        if n % c == 0:
            return c
    raise ValueError(f"no tile for {n}")


def kernel(x_prompt, x_sample, p_prompt, p_sample, state_ssm, state_ssm_conv, state_gdn, state_gdn_conv,
           norm_mix, w_in, ssm_conv_w, ssm_conv_b, ssm_dt_bias, ssm_a_log, ssm_d, ssm_norm,
           gdn_conv_w, gdn_dt_bias, gdn_a_log, gdn_norm, w_branch_ssm, w_branch_gdn, w_out,
           norm_ffn, w_ffn_in, w_ffn_out, norm_pl, w_pl_gate, w_pl_proj, norm_final):
    depth = p_prompt.shape[0]
    bp, seq, d_model = x_prompt.shape
    bs, dec_seq, _ = x_sample.shape
    assert dec_seq == 1 and seq % STEP_TOKENS == 0 and bs % SAMPLE_BB == 0
    h_m, h_g = ssm_dt_bias.shape[1], gdn_dt_bias.shape[1]
    d_inner = h_m * P_M
    tail_rows = slice(SUBLANES - (CONV_K - 1), SUBLANES)

    xp = x_prompt.reshape(bp * seq, d_model)
    xs = x_sample.reshape(bs, d_model)
    new_p = ([], [], [], [])
    new_s = ([], [], [], [])
    for i in range(depth):
        lw = _layer_weights(i, d_model, norm_mix, w_in, ssm_conv_w, ssm_conv_b, ssm_dt_bias, ssm_a_log, ssm_d,
                            ssm_norm, gdn_conv_w, gdn_dt_bias, gdn_a_log, gdn_norm, w_branch_ssm, w_branch_gdn,
                            w_out, norm_ffn, w_ffn_in, w_ffn_out, norm_pl, w_pl_gate, w_pl_proj, norm_final)
        last = i == depth - 1

        tp = bp * seq
        y, st_ssm, cs_ssm = _ssd_prompt(xp, lw["norm_mix"], lw["w_ssd"], bp, seq, **lw["ssm"])
        o, st_gdn, cs_gdn = _gdn_prompt(xp, lw["norm_mix"], lw["w_gdn"], bp, seq, n_heads=h_g, **lw["gdn"])
        xp = _tail(xp, y, o, p_prompt[i].reshape(tp, -1), lw["tail"], _pick_tile(tp, (256, 128)), last)
        new_p[0].append(st_ssm.reshape(bp, h_m, P_M, N_M))
        new_p[1].append(cs_ssm[:, tail_rows, :])
        new_p[2].append(st_gdn)
        new_p[3].append(cs_gdn[:, tail_rows, :])

        ys, cs_s, ss_s = _ssd_sample(_inproj(xs, lw["norm_mix"], lw["w_ssd"]),
                                     jnp.swapaxes(state_ssm_conv[i], 0, 1),
                                     state_ssm[i].reshape(bs, d_inner, N_M), **lw["ssm"])
        os_, cs_g, ss_g = _gdn_sample(_inproj(xs, lw["norm_mix"], lw["w_gdn"]),
                                      jnp.swapaxes(state_gdn_conv[i], 0, 1), state_gdn[i], **lw["gdn"])
        xs = _tail(xs, ys, os_, p_sample[i].reshape(bs, -1), lw["tail"],
                   _pick_tile(bs, (128, 64, 32, 16, 8)), last)
        new_s[0].append(ss_s.reshape(bs, h_m, P_M, N_M))
        new_s[1].append(jnp.swapaxes(cs_s, 0, 1))
        new_s[2].append(ss_g)
        new_s[3].append(jnp.swapaxes(cs_g, 0, 1))

    stack = lambda lst: jnp.stack(lst)
    return (xp.reshape(bp, seq, d_model), xs.reshape(bs, 1, d_model),
            stack(new_p[0]), stack(new_p[1]), stack(new_p[2]), stack(new_p[3]),
            stack(new_s[0]), stack(new_s[1]), stack(new_s[2]), stack(new_s[3]))
```

```python
import functools

import jax
import jax.numpy as jnp
from jax import lax
from jax.experimental import pallas as pl
from jax.experimental.pallas import tpu as pltpu

f32 = jnp.float32
bf16 = jnp.bfloat16

EPS = 1e-6
CONV_K = 4
LANES = 128
SUBLANES = 8
VMEM_LIMIT = 56 * 1024 * 1024

P_M = 64
N_M = 128
G_M = 4
DK = 128
DV = 128
SSD_BLOCK = 128
GDN_CHUNK = 64
GDN_BLOCK = 128
GDN_BASE = 16
STEP_TOKENS = 256
CONV_COLS = 512
SAMPLE_BB = 8


def _nt(a, b):
    return lax.dot_general(a, b, (((1,), (1,)), ((), ())), preferred_element_type=f32)


def _tn(a, b):
    return lax.dot_general(a, b, (((0,), (0,)), ((), ())), preferred_element_type=f32)


def _mm(a, b):
    return jnp.dot(a, b, preferred_element_type=f32)


def _split3(x):
    hi = x.astype(bf16)
    r1 = x - hi.astype(f32)
    mid = r1.astype(bf16)
    lo = (r1 - mid.astype(f32)).astype(bf16)
    return hi, mid, lo


def _mm_sel_rhs(x, sel3):
    return _mm(jnp.concatenate(_split3(x), axis=1), sel3)


def _mm_sel_lhs(sel, x):
    return _mm(jnp.concatenate([sel] * 3, axis=1), jnp.concatenate(_split3(x), axis=0))


def _pack3_lanes(x, width):
    lane = lax.broadcasted_iota(jnp.int32, x.shape, 1)
    hi, mid, lo = (t.astype(f32) for t in _split3(jnp.where(lane < width, x, 0.0)))
    return (hi + pltpu.roll(mid, width, 1) + pltpu.roll(lo, 2 * width, 1)).astype(bf16)


def _silu(x):
    h = 0.5 * x
    return h + h * jnp.tanh(h)


def _softplus(x):
    return jnp.maximum(x, 0.0) + jnp.log1p(jnp.exp(-jnp.abs(x)))


def _rms(x, g):
    return x * lax.rsqrt(jnp.mean(x * x, axis=-1, keepdims=True) + EPS) * g


def _iota2(shape):
    return lax.broadcasted_iota(jnp.int32, shape, 0), lax.broadcasted_iota(jnp.int32, shape, 1)


def _inproj_kernel(x_ref, g_ref, *refs):
    n = len(refs) // 2
    h = _rms(x_ref[...], g_ref[...]).astype(bf16)
    for w_ref, o_ref in zip(refs[:n], refs[n:]):
        o_ref[...] = _mm(h, w_ref[...])


def _inproj(x, g, ws):
    t, d = x.shape
    full = lambda shape: pl.BlockSpec(shape, lambda i: (0, 0))
    return pl.pallas_call(
        _inproj_kernel,
        grid=(1,),
        in_specs=[full((t, d)), full((1, d))] + [full(w.shape) for w in ws],
        out_specs=[full((t, w.shape[1])) for w in ws],
        out_shape=[jax.ShapeDtypeStruct((t, w.shape[1]), f32) for w in ws],
        compiler_params=pltpu.CompilerParams(dimension_semantics=("arbitrary",), vmem_limit_bytes=VMEM_LIMIT),
        name="inproj",
    )(x, g, *ws)


def _project_conv(h, w_ref, ext_ref, act_ref, cw_ref, cb_ref, rows, width):
    slabs = [slice(c0, c0 + CONV_COLS) for c0 in range(0, width, CONV_COLS)]
    u_next = _mm(h, w_ref[:, slabs[0]])
    for k, sl in enumerate(slabs):
        u = u_next
        if k + 1 < len(slabs):
            u_next = _mm(h, w_ref[:, slabs[k + 1]])
        ext_ref[SUBLANES:SUBLANES + rows, sl] = u
        cw = cw_ref[:, sl]
        conv = u * cw[3:4]
        full = ext_ref[:, sl]
        for j in range(CONV_K - 1):
            shifted = pltpu.roll(full, CONV_K - 1 - j, 0)
            conv = conv + shifted[SUBLANES:SUBLANES + rows] * cw[j:j + 1]
        if cb_ref is not None:
            conv = conv + cb_ref[:, sl]
        act_ref[:, sl] = _silu(conv)
        ext_ref[0:SUBLANES, sl] = ext_ref[rows:rows + SUBLANES, sl]


def _ssd_kernel(x_ref, g_ref, wx_ref, wz_ref, wdt_ref, cw_ref, cb_ref, dtb_ref, aneg_ref, dexp_ref, nw_ref, e_ref,
                y_ref, st_out_ref, cs_out_ref, ext_ref, act_ref, z_ref, st_ref, yd_ref):
    i = pl.program_id(1)
    rows = x_ref.shape[0]
    c = SSD_BLOCK
    conv_w = act_ref.shape[1]
    d_inner = y_ref.shape[1]
    gw = d_inner // G_M
    hg = gw // P_M
    n_heads = d_inner // P_M

    @pl.when(i == 0)
    def _():
        ext_ref[0:SUBLANES, :] = jnp.zeros((SUBLANES, conv_w), f32)
        st_ref[...] = jnp.zeros(st_ref.shape, f32)

    h = _rms(x_ref[...], g_ref[...]).astype(bf16)
    _project_conv(h, wx_ref, ext_ref, act_ref, cw_ref, cb_ref, rows, conv_w)
    dt_raw = _mm(h, wdt_ref[...])
    for g in range(G_M):
        z_ref[:, g * gw:(g + 1) * gw] = _mm(h, wz_ref[:, g * gw:(g + 1) * gw])

    r, cc = _iota2((c, c))
    lower = r >= cc
    lower_b = jnp.where(lower, 1.0, 0.0).astype(bf16)
    upper3 = jnp.concatenate([jnp.where(r <= cc, 1.0, 0.0).astype(bf16)] * 3, axis=0)

    for sub in range(rows // c):
        rs = slice(sub * c, (sub + 1) * c)
        dt = _softplus(dt_raw[rs] + dtb_ref[...])
        da = dt * aneg_ref[...]
        acum = _mm_sel_lhs(lower_b, da)
        acum_t = _mm_sel_rhs(da.T, upper3)
        alast = acum[c - 1:c, :]
        dout = jnp.exp(alast - acum)
        scal = _pack3_lanes(jnp.concatenate(
            [dt, dt * dout, jnp.exp(acum), jnp.broadcast_to(jnp.exp(alast), (SUBLANES, LANES))], axis=0), n_heads)
        for g in range(G_M):
            gs = slice(g * gw, (g + 1) * gw)
            se = _mm(scal, e_ref[:, gs])
            dt_e, dd_e, ea_e, cd_e = se[0:c], se[c:2 * c], se[2 * c:3 * c], se[3 * c:3 * c + 1]
            xs = act_ref[rs, gs]
            bm = act_ref[rs, d_inner + g * N_M:d_inner + (g + 1) * N_M].astype(bf16)
            cm = act_ref[rs, d_inner + (G_M + g) * N_M:d_inner + (G_M + g + 1) * N_M].astype(bf16)
            xdt = (xs * dt_e).astype(bf16)
            cb = _nt(cm, bm)
            st = st_ref[:, gs]
            y_off = _mm(cm, st.astype(bf16)) * ea_e
            st_ref[:, gs] = st * cd_e + _tn(bm, (xs * dd_e).astype(bf16))
            for hh in range(hg):
                hd = g * hg + hh
                lmat = jnp.exp(jnp.where(lower, acum[:, hd:hd + 1] - acum_t[hd:hd + 1, :], -jnp.inf))
                yd_ref[:, hh * P_M:(hh + 1) * P_M] = _mm((cb * lmat).astype(bf16), xdt[:, hh * P_M:(hh + 1) * P_M])
            y = yd_ref[...] + y_off + xs * dexp_ref[:, gs]
            y = y * _silu(z_ref[rs, gs])
            y_ref[rs, gs] = _rms(y, nw_ref[:, gs]).astype(y_ref.dtype)

    @pl.when(i == pl.num_programs(1) - 1)
    def _():
        cs_out_ref[0] = ext_ref[0:SUBLANES, :]
        for j in range(d_inner // LANES):
            st_out_ref[0, j * LANES:(j + 1) * LANES, :] = st_ref[:, j * LANES:(j + 1) * LANES].T


def _ssd_prompt(x, g, ws, bsz, seq, cw, cb, dtb, aneg, dexp, nw, e_mat):
    d_model = x.shape[1]
    d_inner = dexp.shape[1]
    conv_w = cw.shape[1]
    rows = STEP_TOKENS
    nb = seq // rows
    tok = lambda b, i: (b * nb + i, 0)
    res = lambda a: pl.BlockSpec(a.shape, lambda b, i: (0, 0), pipeline_mode=pl.Buffered(1))
    return pl.pallas_call(
        _ssd_kernel,
        grid=(bsz, nb),
        in_specs=[pl.BlockSpec((rows, d_model), tok)] + [res(a) for a in (g, *ws, cw, cb, dtb, aneg, dexp, nw, e_mat)],
        out_specs=[
            pl.BlockSpec((rows, d_inner), tok),
            pl.BlockSpec((1, d_inner, N_M), lambda b, i: (b, 0, 0)),
            pl.BlockSpec((1, SUBLANES, conv_w), lambda b, i: (b, 0, 0)),
        ],
        out_shape=[
            jax.ShapeDtypeStruct((bsz * seq, d_inner), bf16),
            jax.ShapeDtypeStruct((bsz, d_inner, N_M), f32),
            jax.ShapeDtypeStruct((bsz, SUBLANES, conv_w), f32),
        ],
        scratch_shapes=[
            pltpu.VMEM((rows + SUBLANES, conv_w), f32),
            pltpu.VMEM((rows, conv_w), f32),
            pltpu.VMEM((rows, d_inner), f32),
            pltpu.VMEM((N_M, d_inner), f32),
            pltpu.VMEM((SSD_BLOCK, d_inner // G_M), f32),
        ],
        compiler_params=pltpu.CompilerParams(
            dimension_semantics=("parallel", "arbitrary"), vmem_limit_bytes=VMEM_LIMIT),
        name="ssd_prompt",
    )(x, g, *ws, cw, cb, dtb, aneg, dexp, nw, e_mat)


def _gdn_kernel(x_ref, g_ref, wx_ref, wgate_ref, wba_ref, cw_ref, dtb_ref, aneg_ref, nw_ref,
                o_ref, st_out_ref, cs_out_ref, ext_ref, act_ref, gate_ref, st_ref):
    i = pl.program_id(1)
    rows = x_ref.shape[0]
    blk = GDN_BLOCK
    ck = GDN_CHUNK
    conv_w = act_ref.shape[1]
    n_heads = st_ref.shape[0]
    hk = n_heads * DK
    per_blk = blk // ck
    heads = range(n_heads)

    @pl.when(i == 0)
    def _():
        ext_ref[0:SUBLANES, :] = jnp.zeros((SUBLANES, conv_w), f32)
        st_ref[...] = jnp.zeros(st_ref.shape, f32)

    h_in = _rms(x_ref[...], g_ref[...]).astype(bf16)
    _project_conv(h_in, wx_ref, ext_ref, act_ref, cw_ref, None, rows, conv_w)
    ba = _mm(h_in, wba_ref[...])
    gate_ref[...] = _mm(h_in, wgate_ref[...])

    r, cc = _iota2((blk, blk))
    same = (r // ck) == (cc // ck)
    lower = same & (r >= cc)
    strict = same & (r > cc)
    lower_b = jnp.where(lower, 1.0, 0.0).astype(bf16)
    upper3 = jnp.concatenate([jnp.where(same & (r <= cc), 1.0, 0.0).astype(bf16)] * 3, axis=0)
    rcol = r[:, 0:1]
    eye = jnp.where(r == cc, 1.0, 0.0)
    blk_masks = []
    size = GDN_BASE
    while size <= ck:
        blk_masks.append((r // size) == (cc // size))
        size *= 2

    n_sub = rows // blk
    prep = [None] * n_sub
    xs = [None] * n_sub
    qks, qgs, kgs, gls = ([None] * (n_sub * n_heads) for _ in range(4))

    def prepare(sub):
        rs = slice(sub * blk, (sub + 1) * blk)
        beta = jax.nn.sigmoid(ba[rs, :LANES])
        glog = aneg_ref[...] * _softplus(ba[rs, LANES:] + dtb_ref[...])
        gcum = _mm_sel_lhs(lower_b, glog)
        gcum_t = _mm_sel_rhs(glog.T, upper3)
        nmats, rhs = [], []
        for h in heads:
            q = act_ref[rs, h * DK:(h + 1) * DK]
            k = act_ref[rs, hk + h * DK:hk + (h + 1) * DK]
            v = act_ref[rs, 2 * hk + h * DV:2 * hk + (h + 1) * DV]
            qn = q * lax.rsqrt(jnp.sum(q * q, axis=-1, keepdims=True) + EPS) * (DK ** -0.5)
            kn = k * lax.rsqrt(jnp.sum(k * k, axis=-1, keepdims=True) + EPS)
            bcol = beta[:, h:h + 1]
            gcol = gcum[:, h:h + 1]
            eg = jnp.exp(gcol)
            dm = jnp.exp(jnp.where(lower, gcol - gcum_t[h:h + 1, :], -jnp.inf))
            knb = kn.astype(bf16)
            kb = kn * bcol
            nmats.append(jnp.where(strict, -(_nt(kb.astype(bf16), knb) * dm), 0.0))
            rhs.append(jnp.concatenate([v * bcol, kb * eg], axis=-1))
            glast = [gcol[(j + 1) * ck - 1:(j + 1) * ck] for j in range(per_blk)]
            glast_row = glast[per_blk - 1]
            for j in range(per_blk - 2, -1, -1):
                glast_row = jnp.where(rcol < (j + 1) * ck, glast[j], glast_row)
            idx = sub * n_heads + h
            qks[idx] = jnp.where(lower, _nt(qn.astype(bf16), knb) * dm, 0.0).astype(bf16)
            qgs[idx] = (qn * eg).astype(bf16)
            kgs[idx] = (kn * jnp.exp(glast_row - gcol)).astype(bf16)
            gls[idx] = [jnp.exp(gl) for gl in glast]
            yield
        prep[sub] = (nmats, rhs)

    def solve(sub):
        nmats, rhs = prep[sub]
        qs = [jnp.where(blk_masks[0], n, 0.0) for n in nmats]
        minv = [eye + q for q in qs]
        span = 2
        while span < GDN_BASE:
            qbs = [q.astype(bf16) for q in qs]
            qs = [_mm(qb, qb) for qb in qbs]
            yield
            minv = [m + _mm(m.astype(bf16), q.astype(bf16)) for m, q in zip(minv, qs)]
            yield
            span *= 2
        for lvl in range(1, len(blk_masks)):
            off = blk_masks[lvl] & jnp.logical_not(blk_masks[lvl - 1])
            mbs = [m.astype(bf16) for m in minv]
            ems = [_mm(jnp.where(off, n, 0.0).astype(bf16), mb) for n, mb in zip(nmats, mbs)]
            yield
            minv = [m + _mm(mb, em.astype(bf16)) for m, mb, em in zip(minv, mbs, ems)]
            yield
        xs[sub] = [_mm(m.astype(bf16), x.astype(bf16)) for m, x in zip(minv, rhs)]

    state = {"s": [st_ref[h] for h in heads]}
    zeros = jnp.zeros((ck, DV), bf16)

    def recur(sub):
        for j in range(per_blk):
            rl = slice(j * ck, (j + 1) * ck)
            ro = slice(sub * blk + j * ck, sub * blk + (j + 1) * ck)
            at = lambda lst, h: lst[sub * n_heads + h]
            ss = state["s"]
            sbs = [s.astype(bf16) for s in ss]
            vns = [xs[sub][h][rl, :DV] - _mm(xs[sub][h][rl, DV:].astype(bf16), sbs[h]) for h in heads]
            yield
            vfull = [jnp.concatenate([zeros] * j + [vns[h].astype(bf16)] + [zeros] * (per_blk - 1 - j), axis=0)
                     for h in heads]
            outs = [_mm(at(qgs, h)[rl], sbs[h]) + _mm(at(qks, h)[rl, :], vfull[h]) for h in heads]
            yield
            state["s"] = [ss[h] * at(gls, h)[j] + _tn(at(kgs, h)[rl], vns[h].astype(bf16)) for h in heads]
            yield
            for h in heads:
                gt = gate_ref[ro, h * DV:(h + 1) * DV]
                o_ref[ro, h * DV:(h + 1) * DV] = (_rms(outs[h], nw_ref[...]) * _silu(gt)).astype(o_ref.dtype)
            yield

    def emit(*gens):
        live = list(gens)
        while live:
            for gen in list(live):
                if next(gen, StopIteration) is StopIteration:
                    live.remove(gen)

    emit(prepare(0))
    for sub in range(n_sub):
        later = [prepare(sub + 1)] if sub + 1 < n_sub else []
        earlier = [recur(sub - 1)] if sub > 0 else []
        emit(solve(sub), *later, *earlier)
    emit(recur(n_sub - 1))
    for h in heads:
        st_ref[h] = state["s"][h]

    @pl.when(i == pl.num_programs(1) - 1)
    def _():
        cs_out_ref[0] = ext_ref[0:SUBLANES, :]
        st_out_ref[0] = st_ref[...]


def _gdn_prompt(x, g, ws, bsz, seq, cw, dtb, aneg, nw, n_heads):
    d_model = x.shape[1]
    conv_w = cw.shape[1]
    hv = n_heads * DV
    rows = STEP_TOKENS
    nb = seq // rows
    tok = lambda b, i: (b * nb + i, 0)
    res = lambda a: pl.BlockSpec(a.shape, lambda b, i: (0, 0), pipeline_mode=pl.Buffered(1))
    return pl.pallas_call(
        _gdn_kernel,
        grid=(bsz, nb),
        in_specs=[pl.BlockSpec((rows, d_model), tok)] + [res(a) for a in (g, *ws, cw, dtb, aneg, nw)],
        out_specs=[
            pl.BlockSpec((rows, hv), tok),
            pl.BlockSpec((1, n_heads, DK, DV), lambda b, i: (b, 0, 0, 0)),
            pl.BlockSpec((1, SUBLANES, conv_w), lambda b, i: (b, 0, 0)),
        ],
        out_shape=[
            jax.ShapeDtypeStruct((bsz * seq, hv), bf16),
            jax.ShapeDtypeStruct((bsz, n_heads, DK, DV), f32),
            jax.ShapeDtypeStruct((bsz, SUBLANES, conv_w), f32),
        ],
        scratch_shapes=[
            pltpu.VMEM((rows + SUBLANES, conv_w), f32),
            pltpu.VMEM((rows, conv_w), f32),
            pltpu.VMEM((rows, hv), f32),
            pltpu.VMEM((n_heads, DK, DV), f32),
        ],
        compiler_params=pltpu.CompilerParams(
            dimension_semantics=("parallel", "arbitrary"), vmem_limit_bytes=VMEM_LIMIT),
        name="gdn_prompt",
    )(x, g, *ws, cw, dtb, aneg, nw)


def _conv_step(u, cs_ref, cs_out_ref, cw_ref, cb_ref):
    cw = cw_ref[...]
    conv = u * cw[3:4]
    for j in range(CONV_K - 1):
        conv = conv + cs_ref[j] * cw[j:j + 1]
    if cb_ref is not None:
        conv = conv + cb_ref[...]
    for j in range(CONV_K - 2):
        cs_out_ref[j] = cs_ref[j + 1]
    cs_out_ref[CONV_K - 2] = u
    return _silu(conv)


def _ssd_step_kernel(xbc_ref, z_ref, dt_ref, cs_ref, st_ref, cw_ref, cb_ref, dtb_ref, aneg_ref, dexp_ref,
                     nw_ref, e_ref, y_ref, cs_out_ref, st_out_ref, yacc_ref):
    bb = xbc_ref.shape[0]
    d_inner = y_ref.shape[1]
    gw = d_inner // G_M
    act = _conv_step(xbc_ref[...], cs_ref, cs_out_ref, cw_ref, cb_ref)
    z = z_ref[...]
    dt = _softplus(dt_ref[...] + dtb_ref[...])
    dec = jnp.exp(dt * aneg_ref[...])
    se = _mm(_pack3_lanes(jnp.concatenate([dt, dec], axis=0), d_inner // P_M), e_ref[...])
    xs = act[:, :d_inner]
    xdt = xs * se[0:bb]
    dec_e = se[bb:2 * bb]
    d3 = jnp.concatenate(_split3(dec_e), axis=0)
    r3, c3 = _iota2((3 * bb, bb * N_M))
    seq3 = c3 // N_M
    onehot3 = jnp.where((r3 == seq3) | (r3 == seq3 + bb) | (r3 == seq3 + 2 * bb), 1.0, 0.0).astype(bf16)
    rb, cb_i = _iota2((bb, bb * N_M))
    own = rb == cb_i // N_M
    rowid = rb[:, 0:1]

    for g in range(G_M):
        gs = slice(g * gw, (g + 1) * gw)
        bm = act[:, d_inner + g * N_M:d_inner + (g + 1) * N_M]
        cm = act[:, d_inner + (G_M + g) * N_M:d_inner + (G_M + g + 1) * N_M].astype(bf16)
        bdiag = jnp.where(own, jnp.concatenate([bm] * bb, axis=1), 0.0).astype(bf16)
        dcol = _tn(d3[:, gs], onehot3)
        outer = _tn(xdt[:, gs].astype(bf16), bdiag)
        yg = None
        for b in range(bb):
            bs = slice(b * N_M, (b + 1) * N_M)
            s_new = st_ref[b, gs, :] * dcol[:, bs] + outer[:, bs]
            st_out_ref[b, gs, :] = s_new
            yb = _nt(cm, s_new.astype(bf16))
            yg = yb if yg is None else jnp.where(rowid == b, yb, yg)
        yacc_ref[:, gs] = yg

    y = yacc_ref[...] + xs * dexp_ref[...]
    y = y * _silu(z)
    for g in range(G_M):
        gs = slice(g * gw, (g + 1) * gw)
        y_ref[:, gs] = _rms(y[:, gs], nw_ref[:, gs]).astype(y_ref.dtype)


def _ssd_sample(projs, conv_state, state, cw, cb, dtb, aneg, dexp, nw, e_mat):
    bsz = projs[0].shape[0]
    d_inner = dexp.shape[1]
    conv_w = cw.shape[1]
    bb = SAMPLE_BB
    const = lambda a: pl.BlockSpec(a.shape, lambda i: (0, 0))
    return pl.pallas_call(
        _ssd_step_kernel,
        grid=(bsz // bb,),
        in_specs=[pl.BlockSpec((bb, p.shape[1]), lambda i: (i, 0)) for p in projs] + [
            pl.BlockSpec((CONV_K - 1, bb, conv_w), lambda i: (0, i, 0)),
            pl.BlockSpec((bb, d_inner, N_M), lambda i: (i, 0, 0)),
        ] + [const(a) for a in (cw, cb, dtb, aneg, dexp, nw, e_mat)],
        out_specs=[
            pl.BlockSpec((bb, d_inner), lambda i: (i, 0)),
            pl.BlockSpec((CONV_K - 1, bb, conv_w), lambda i: (0, i, 0)),
            pl.BlockSpec((bb, d_inner, N_M), lambda i: (i, 0, 0)),
        ],
        out_shape=[
            jax.ShapeDtypeStruct((bsz, d_inner), bf16),
            jax.ShapeDtypeStruct((CONV_K - 1, bsz, conv_w), f32),
            jax.ShapeDtypeStruct((bsz, d_inner, N_M), f32),
        ],
        scratch_shapes=[pltpu.VMEM((bb, d_inner), f32)],
        compiler_params=pltpu.CompilerParams(
            dimension_semantics=("parallel",), vmem_limit_bytes=VMEM_LIMIT),
        name="ssd_sample",
    )(*projs, conv_state, state, cw, cb, dtb, aneg, dexp, nw, e_mat)


def _gdn_step_kernel(qkv_ref, gate_ref, ba_ref, cs_ref, st_ref, cw_ref, dtb_ref, aneg_ref, nw_ref,
                     o_ref, cs_out_ref, st_out_ref):
    bb = qkv_ref.shape[0]
    n_heads = st_ref.shape[1]
    hk = n_heads * DK
    act = _conv_step(qkv_ref[...], cs_ref, cs_out_ref, cw_ref, None)
    beta = jax.nn.sigmoid(ba_ref[:, :LANES])
    eg = jnp.exp(aneg_ref[...] * _softplus(ba_ref[:, LANES:] + dtb_ref[...]))
    rb, cb_i = _iota2((bb, bb * DV))
    own = rb == cb_i // DV
    rowid = rb[:, 0:1]

    for h in range(n_heads):
        q = act[:, h * DK:(h + 1) * DK]
        k = act[:, hk + h * DK:hk + (h + 1) * DK]
        v = act[:, 2 * hk + h * DV:2 * hk + (h + 1) * DV]
        qn = q * lax.rsqrt(jnp.sum(q * q, axis=-1, keepdims=True) + EPS) * (DK ** -0.5)
        kn = k * lax.rsqrt(jnp.sum(k * k, axis=-1, keepdims=True) + EPS)
        bcol = beta[:, h:h + 1]
        ecol = eg[:, h:h + 1]
        kb = kn * bcol
        u = v * bcol
        qk = jnp.sum(qn.astype(bf16).astype(f32) * kn.astype(bf16).astype(f32), axis=-1, keepdims=True)
        wq = jnp.concatenate([(kb * ecol).astype(bf16), (qn * ecol).astype(bf16)], axis=0)
        vn = o = None
        for b in range(bb):
            ws = _mm(wq, st_ref[b, h].astype(bf16))
            vn_b = u - ws[:bb]
            o_b = ws[bb:] + qk * vn_b
            vn = vn_b if vn is None else jnp.where(rowid == b, vn_b, vn)
            o = o_b if o is None else jnp.where(rowid == b, o_b, o)
        vdiag = jnp.where(own, jnp.concatenate([vn] * bb, axis=1), 0.0).astype(bf16)
        outer = _tn(kn.astype(bf16), vdiag)
        for b in range(bb):
            st_out_ref[b, h] = st_ref[b, h] * ecol[b:b + 1, :] + outer[:, b * DV:(b + 1) * DV]
        gt = gate_ref[:, h * DV:(h + 1) * DV]
        o_ref[:, h * DV:(h + 1) * DV] = (_rms(o, nw_ref[...]) * _silu(gt)).astype(o_ref.dtype)


def _gdn_sample(projs, conv_state, state, cw, dtb, aneg, nw):
    bsz = projs[0].shape[0]
    n_heads = state.shape[1]
    conv_w = cw.shape[1]
    hv = n_heads * DV
    bb = SAMPLE_BB
    const = lambda a: pl.BlockSpec(a.shape, lambda i: (0, 0))
    return pl.pallas_call(
        _gdn_step_kernel,
        grid=(bsz // bb,),
        in_specs=[pl.BlockSpec((bb, p.shape[1]), lambda i: (i, 0)) for p in projs] + [
            pl.BlockSpec((CONV_K - 1, bb, conv_w), lambda i: (0, i, 0)),
            pl.BlockSpec((bb, n_heads, DK, DV), lambda i: (i, 0, 0, 0)),
        ] + [const(a) for a in (cw, dtb, aneg, nw)],
        out_specs=[
            pl.BlockSpec((bb, hv), lambda i: (i, 0)),
            pl.BlockSpec((CONV_K - 1, bb, conv_w), lambda i: (0, i, 0)),
            pl.BlockSpec((bb, n_heads, DK, DV), lambda i: (i, 0, 0, 0)),
        ],
        out_shape=[
            jax.ShapeDtypeStruct((bsz, hv), bf16),
            jax.ShapeDtypeStruct((CONV_K - 1, bsz, conv_w), f32),
            jax.ShapeDtypeStruct((bsz, n_heads, DK, DV), f32),
        ],
        compiler_params=pltpu.CompilerParams(
            dimension_semantics=("parallel",), vmem_limit_bytes=VMEM_LIMIT),
        name="gdn_sample",
    )(*projs, conv_state, state, cw, dtb, aneg, nw)


def _tail_kernel(x_ref, y_ref, o_ref, p_ref, nmix_ref, wmg_ref, wbs_ref, wbg_ref, wout_ref, nf_ref, wfi_ref,
                 wfo_ref, npl_ref, wpg_ref, wpp_ref, nfin_ref, out_ref, *, ff_chunks, final_norm):
    d = x_ref.shape[1]
    d_ff = wfo_ref.shape[0]
    x = x_ref[...]
    mg = _mm(_rms(x, nmix_ref[...]).astype(bf16), wmg_ref[...])
    mix = (jax.nn.sigmoid(mg[:, :d]) * _mm(y_ref[...], wbs_ref[...])
           + jax.nn.sigmoid(mg[:, d:]) * _mm(o_ref[...], wbg_ref[...]))
    x = x + _mm(mix.astype(bf16), wout_ref[...])
    h = _rms(x, nf_ref[...]).astype(bf16)
    fc = d_ff // ff_chunks
    for c in range(ff_chunks):
        gt = _mm(h, wfi_ref[:, c * fc:(c + 1) * fc])
        up = _mm(h, wfi_ref[:, d_ff + c * fc:d_ff + (c + 1) * fc])
        x = x + _mm((_silu(gt) * up).astype(bf16), wfo_ref[c * fc:(c + 1) * fc, :])
    pe = _mm(p_ref[...].astype(bf16), wpp_ref[...])
    x = x + pe * jax.nn.sigmoid(_mm(_rms(x, npl_ref[...]).astype(bf16), wpg_ref[...]))
    if final_norm:
        x = _rms(x, nfin_ref[...])
    out_ref[...] = x


def _tail(x, y, o, p, wts, tm, final_norm):
    t, d = x.shape
    tok = lambda w: pl.BlockSpec((tm, w), lambda i: (i, 0))
    res = lambda a: pl.BlockSpec(a.shape, lambda i: (0, 0), pipeline_mode=pl.Buffered(1))
    names = ("nmix", "wmg", "wbs", "wbg", "wout", "nf", "wfi", "wfo", "npl", "wpg", "wpp", "nfin")
    return pl.pallas_call(
        functools.partial(_tail_kernel, ff_chunks=2, final_norm=final_norm),
        grid=(t // tm,),
        in_specs=[tok(d), tok(y.shape[1]), tok(o.shape[1]), tok(p.shape[1])] + [res(wts[n]) for n in names],
        out_specs=tok(d),
        out_shape=jax.ShapeDtypeStruct((t, d), f32),
        compiler_params=pltpu.CompilerParams(
            dimension_semantics=("parallel",), vmem_limit_bytes=VMEM_LIMIT),
        name="tail",
    )(x, y, o, p, *[wts[n] for n in names])


def _pad_lanes(v, width=LANES):
    return jnp.pad(v, ((0, 0), (0, width - v.shape[1])))


def _layer_weights(i, d_model, norm_mix, w_in, ssm_conv_w, ssm_conv_b, ssm_dt_bias, ssm_a_log, ssm_d, ssm_norm,
                   gdn_conv_w, gdn_dt_bias, gdn_a_log, gdn_norm, w_branch_ssm, w_branch_gdn, w_out,
                   norm_ffn, w_ffn_in, w_ffn_out, norm_pl, w_pl_gate, w_pl_proj, norm_final):
    h_m = ssm_dt_bias.shape[1]
    h_g = gdn_dt_bias.shape[1]
    d_inner = h_m * P_M
    conv_m = ssm_conv_w.shape[2]
    conv_g = gdn_conv_w.shape[2]
    sizes = (d_inner, conv_m, h_m, conv_g, h_g * DV, h_g, h_g, 2 * d_model)
    starts = [0]
    for s in sizes:
        starts.append(starts[-1] + s)
    seg = lambda j: w_in[i][:, starts[j]:starts[j + 1]]
    z, xbc, dtw, qkv, gate, bw, aw, mg = (seg(j) for j in range(8))
    w_ssd = (xbc.astype(bf16), z.astype(bf16), _pad_lanes(dtw).astype(bf16))
    w_gdn = (qkv.astype(bf16), gate.astype(bf16),
             jnp.concatenate([_pad_lanes(bw), _pad_lanes(aw)], axis=1).astype(bf16))
    head_of = jnp.arange(d_inner, dtype=jnp.int32) // P_M
    assert 3 * h_m <= LANES
    rows_e = jnp.arange(LANES, dtype=jnp.int32)[:, None]
    e_mat = ((rows_e % h_m == head_of[None, :]) & (rows_e < 3 * h_m)).astype(bf16)
    row = lambda v: v.reshape(1, -1).astype(f32)
    return dict(
        norm_mix=row(norm_mix[i]), w_ssd=w_ssd, w_gdn=w_gdn,
        ssm=dict(cw=ssm_conv_w[i], cb=row(ssm_conv_b[i]), dtb=_pad_lanes(row(ssm_dt_bias[i])),
                 aneg=_pad_lanes(-jnp.exp(row(ssm_a_log[i]))), dexp=row(jnp.repeat(ssm_d[i], P_M)),
                 nw=row(ssm_norm[i]), e_mat=e_mat),
        gdn=dict(cw=gdn_conv_w[i], dtb=_pad_lanes(row(gdn_dt_bias[i])),
                 aneg=_pad_lanes(-jnp.exp(row(gdn_a_log[i]))), nw=row(gdn_norm[i])),
        tail=dict(nmix=row(norm_mix[i]), wmg=mg.astype(bf16),
                  wbs=w_branch_ssm[i].astype(bf16), wbg=w_branch_gdn[i].astype(bf16), wout=w_out[i].astype(bf16),
                  nf=row(norm_ffn[i]), wfi=w_ffn_in[i].astype(bf16), wfo=w_ffn_out[i].astype(bf16),
                  npl=row(norm_pl[i]), wpg=w_pl_gate[i].astype(bf16), wpp=w_pl_proj[i].astype(bf16),
                  nfin=row(norm_final)),
    )


def _pick_tile(n, candidates):
    for c in candidates:
        if n % c == 0:
            return c
    raise ValueError(f"no tile for {n}")


def kernel(x_prompt, x_sample, p_prompt, p_sample, state_ssm, state_ssm_conv, state_gdn, state_gdn_conv,
           norm_mix, w_in, ssm_conv_w, ssm_conv_b, ssm_dt_bias, ssm_a_log, ssm_d, ssm_norm,
           gdn_conv_w, gdn_dt_bias, gdn_a_log, gdn_norm, w_branch_ssm, w_branch_gdn, w_out,
           norm_ffn, w_ffn_in, w_ffn_out, norm_pl, w_pl_gate, w_pl_proj, norm_final):
    depth = p_prompt.shape[0]
    bp, seq, d_model = x_prompt.shape
    bs, dec_seq, _ = x_sample.shape
    assert dec_seq == 1 and seq % STEP_TOKENS == 0 and bs % SAMPLE_BB == 0
    h_m, h_g = ssm_dt_bias.shape[1], gdn_dt_bias.shape[1]
    d_inner = h_m * P_M
    tail_rows = slice(SUBLANES - (CONV_K - 1), SUBLANES)

    xp = x_prompt.reshape(bp * seq, d_model)
    xs = x_sample.reshape(bs, d_model)
    new_p = ([], [], [], [])
    new_s = ([], [], [], [])
    for i in range(depth):
        lw = _layer_weights(i, d_model, norm_mix, w_in, ssm_conv_w, ssm_conv_b, ssm_dt_bias, ssm_a_log, ssm_d,
                            ssm_norm, gdn_conv_w, gdn_dt_bias, gdn_a_log, gdn_norm, w_branch_ssm, w_branch_gdn,
                            w_out, norm_ffn, w_ffn_in, w_ffn_out, norm_pl, w_pl_gate, w_pl_proj, norm_final)
        last = i == depth - 1

        tp = bp * seq
        y, st_ssm, cs_ssm = _ssd_prompt(xp, lw["norm_mix"], lw["w_ssd"], bp, seq, **lw["ssm"])
        o, st_gdn, cs_gdn = _gdn_prompt(xp, lw["norm_mix"], lw["w_gdn"], bp, seq, n_heads=h_g, **lw["gdn"])
        xp = _tail(xp, y, o, p_prompt[i].reshape(tp, -1), lw["tail"], _pick_tile(tp, (256, 128)), last)
        new_p[0].append(st_ssm.reshape(bp, h_m, P_M, N_M))
        new_p[1].append(cs_ssm[:, tail_rows, :])
        new_p[2].append(st_gdn)
        new_p[3].append(cs_gdn[:, tail_rows, :])

        ys, cs_s, ss_s = _ssd_sample(_inproj(xs, lw["norm_mix"], lw["w_ssd"]),
                                     jnp.swapaxes(state_ssm_conv[i], 0, 1),
                                     state_ssm[i].reshape(bs, d_inner, N_M), **lw["ssm"])
        os_, cs_g, ss_g = _gdn_sample(_inproj(xs, lw["norm_mix"], lw["w_gdn"]),
                                      jnp.swapaxes(state_gdn_conv[i], 0, 1), state_gdn[i], **lw["gdn"])
        xs = _tail(xs, ys, os_, p_sample[i].reshape(bs, -1), lw["tail"],
                   _pick_tile(bs, (128, 64, 32, 16, 8)), last)
        new_s[0].append(ss_s.reshape(bs, h_m, P_M, N_M))
        new_s[1].append(jnp.swapaxes(cs_s, 0, 1))
        new_s[2].append(ss_g)
        new_s[3].append(jnp.swapaxes(cs_g, 0, 1))

    stack = lambda lst: jnp.stack(lst)
    return (xp.reshape(bp, seq, d_model), xs.reshape(bs, 1, d_model),
            stack(new_p[0]), stack(new_p[1]), stack(new_p[2]), stack(new_p[3]),
            stack(new_s[0]), stack(new_s[1]), stack(new_s[2]), stack(new_s[3]))
```

```python
import functools

import jax
import jax.numpy as jnp
from jax import lax
from jax.experimental import pallas as pl
from jax.experimental.pallas import tpu as pltpu

f32 = jnp.float32
bf16 = jnp.bfloat16

EPS = 1e-6
CONV_K = 4
LANES = 128
SUBLANES = 8
VMEM_LIMIT = 56 * 1024 * 1024

P_M = 64
N_M = 128
G_M = 4
DK = 128
DV = 128
SSD_BLOCK = 128
GDN_CHUNK = 64
GDN_BLOCK = 128
GDN_BASE = 16
STEP_TOKENS = 512
CONV_COLS = 512
SAMPLE_BB = 8
SAMPLE_BB_GDN = 16


def _nt(a, b):
    return lax.dot_general(a, b, (((1,), (1,)), ((), ())), preferred_element_type=f32)


def _tn(a, b):
    return lax.dot_general(a, b, (((0,), (0,)), ((), ())), preferred_element_type=f32)


def _mm(a, b):
    return jnp.dot(a, b, preferred_element_type=f32)


def _split3(x):
    hi = x.astype(bf16)
    r1 = x - hi.astype(f32)
    mid = r1.astype(bf16)
    lo = (r1 - mid.astype(f32)).astype(bf16)
    return hi, mid, lo


def _mm_sel_rhs(x, sel3):
    return _mm(jnp.concatenate(_split3(x), axis=1), sel3)


def _mm_sel_lhs(sel, x):
    return _mm(jnp.concatenate([sel] * 3, axis=1), jnp.concatenate(_split3(x), axis=0))


def _pack3_lanes(x, width):
    lane = lax.broadcasted_iota(jnp.int32, x.shape, 1)
    hi, mid, lo = (t.astype(f32) for t in _split3(jnp.where(lane < width, x, 0.0)))
    return (hi + pltpu.roll(mid, width, 1) + pltpu.roll(lo, 2 * width, 1)).astype(bf16)


def _silu(x):
    h = 0.5 * x
    return h + h * jnp.tanh(h)


def _softplus(x):
    return jnp.maximum(x, 0.0) + jnp.log1p(jnp.exp(-jnp.abs(x)))


def _rms(x, g):
    return x * lax.rsqrt(jnp.mean(x * x, axis=-1, keepdims=True) + EPS) * g


def _iota2(shape):
    return lax.broadcasted_iota(jnp.int32, shape, 0), lax.broadcasted_iota(jnp.int32, shape, 1)


def _inproj_kernel(x_ref, g_ref, *refs):
    n = len(refs) // 2
    h = _rms(x_ref[...], g_ref[...]).astype(bf16)
    for w_ref, o_ref in zip(refs[:n], refs[n:]):
        o_ref[...] = _mm(h, w_ref[...])


def _inproj(x, g, ws):
    t, d = x.shape
    full = lambda shape: pl.BlockSpec(shape, lambda i: (0, 0))
    return pl.pallas_call(
        _inproj_kernel,
        grid=(1,),
        in_specs=[full((t, d)), full((1, d))] + [full(w.shape) for w in ws],
        out_specs=[full((t, w.shape[1])) for w in ws],
        out_shape=[jax.ShapeDtypeStruct((t, w.shape[1]), f32) for w in ws],
        compiler_params=pltpu.CompilerParams(dimension_semantics=("arbitrary",), vmem_limit_bytes=VMEM_LIMIT),
        name="inproj",
    )(x, g, *ws)


def _project_conv(h, w_ref, ext_ref, act_ref, cw_ref, cb_ref, rows, width):
    slabs = [slice(c0, c0 + CONV_COLS) for c0 in range(0, width, CONV_COLS)]
    u_next = _mm(h, w_ref[:, slabs[0]])
    for k, sl in enumerate(slabs):
        u = u_next
        if k + 1 < len(slabs):
            u_next = _mm(h, w_ref[:, slabs[k + 1]])
        ext_ref[SUBLANES:SUBLANES + rows, sl] = u
        cw = cw_ref[:, sl]
        conv = u * cw[3:4]
        full = ext_ref[:, sl]
        for j in range(CONV_K - 1):
            shifted = pltpu.roll(full, CONV_K - 1 - j, 0)
            conv = conv + shifted[SUBLANES:SUBLANES + rows] * cw[j:j + 1]
        if cb_ref is not None:
            conv = conv + cb_ref[:, sl]
        act_ref[:, sl] = _silu(conv)
        ext_ref[0:SUBLANES, sl] = ext_ref[rows:rows + SUBLANES, sl]


def _ssd_kernel(x_ref, g_ref, wx_ref, wz_ref, wdt_ref, cw_ref, cb_ref, dtb_ref, aneg_ref, dexp_ref, nw_ref, e_ref,
                y_ref, st_out_ref, cs_out_ref, ext_ref, act_ref, z_ref, st_ref, yd_ref):
    i = pl.program_id(1)
    rows = x_ref.shape[0]
    c = SSD_BLOCK
    conv_w = act_ref.shape[1]
    d_inner = y_ref.shape[1]
    gw = d_inner // G_M
    hg = gw // P_M
    n_heads = d_inner // P_M

    @pl.when(i == 0)
    def _():
        ext_ref[0:SUBLANES, :] = jnp.zeros((SUBLANES, conv_w), f32)
        st_ref[...] = jnp.zeros(st_ref.shape, f32)

    h = _rms(x_ref[...], g_ref[...]).astype(bf16)
    _project_conv(h, wx_ref, ext_ref, act_ref, cw_ref, cb_ref, rows, conv_w)
    dt_raw = _mm(h, wdt_ref[...])
    for g in range(G_M):
        z_ref[:, g * gw:(g + 1) * gw] = _mm(h, wz_ref[:, g * gw:(g + 1) * gw])

    r, cc = _iota2((c, c))
    lower = r >= cc
    lower_b = jnp.where(lower, 1.0, 0.0).astype(bf16)
    upper3 = jnp.concatenate([jnp.where(r <= cc, 1.0, 0.0).astype(bf16)] * 3, axis=0)

    for sub in range(rows // c):
        rs = slice(sub * c, (sub + 1) * c)
        dt = _softplus(dt_raw[rs] + dtb_ref[...])
        da = dt * aneg_ref[...]
        acum = _mm_sel_lhs(lower_b, da)
        acum_t = _mm_sel_rhs(da.T, upper3)
        alast = acum[c - 1:c, :]
        dout = jnp.exp(alast - acum)
        scal = _pack3_lanes(jnp.concatenate(
            [dt, dt * dout, jnp.exp(acum), jnp.broadcast_to(jnp.exp(alast), (SUBLANES, LANES))], axis=0), n_heads)
        for g in range(G_M):
            gs = slice(g * gw, (g + 1) * gw)
            se = _mm(scal, e_ref[:, gs])
            dt_e, dd_e, ea_e, cd_e = se[0:c], se[c:2 * c], se[2 * c:3 * c], se[3 * c:3 * c + 1]
            xs = act_ref[rs, gs]
            bm = act_ref[rs, d_inner + g * N_M:d_inner + (g + 1) * N_M].astype(bf16)
            cm = act_ref[rs, d_inner + (G_M + g) * N_M:d_inner + (G_M + g + 1) * N_M].astype(bf16)
            xdt = (xs * dt_e).astype(bf16)
            cb = _nt(cm, bm)
            st = st_ref[:, gs]
            y_off = _mm(cm, st.astype(bf16)) * ea_e
            st_ref[:, gs] = st * cd_e + _tn(bm, (xs * dd_e).astype(bf16))
            for hh in range(hg):
                hd = g * hg + hh
                lmat = jnp.exp(jnp.where(lower, acum[:, hd:hd + 1] - acum_t[hd:hd + 1, :], -jnp.inf))
                yd_ref[:, hh * P_M:(hh + 1) * P_M] = _mm((cb * lmat).astype(bf16), xdt[:, hh * P_M:(hh + 1) * P_M])
            y = yd_ref[...] + y_off + xs * dexp_ref[:, gs]
            y = y * _silu(z_ref[rs, gs])
            y_ref[rs, gs] = _rms(y, nw_ref[:, gs]).astype(y_ref.dtype)

    @pl.when(i == pl.num_programs(1) - 1)
    def _():
        cs_out_ref[0] = ext_ref[0:SUBLANES, :]
        for j in range(d_inner // LANES):
            st_out_ref[0, j * LANES:(j + 1) * LANES, :] = st_ref[:, j * LANES:(j + 1) * LANES].T


def _ssd_prompt(x, g, ws, bsz, seq, cw, cb, dtb, aneg, dexp, nw, e_mat):
    d_model = x.shape[1]
    d_inner = dexp.shape[1]
    conv_w = cw.shape[1]
    rows = STEP_TOKENS
    nb = seq // rows
    tok = lambda b, i: (b * nb + i, 0)
    res = lambda a: pl.BlockSpec(a.shape, lambda b, i: (0, 0), pipeline_mode=pl.Buffered(1))
    return pl.pallas_call(
        _ssd_kernel,
        grid=(bsz, nb),
        in_specs=[pl.BlockSpec((rows, d_model), tok)] + [res(a) for a in (g, *ws, cw, cb, dtb, aneg, dexp, nw, e_mat)],
        out_specs=[
            pl.BlockSpec((rows, d_inner), tok),
            pl.BlockSpec((1, d_inner, N_M), lambda b, i: (b, 0, 0)),
            pl.BlockSpec((1, SUBLANES, conv_w), lambda b, i: (b, 0, 0)),
        ],
        out_shape=[
            jax.ShapeDtypeStruct((bsz * seq, d_inner), bf16),
            jax.ShapeDtypeStruct((bsz, d_inner, N_M), f32),
            jax.ShapeDtypeStruct((bsz, SUBLANES, conv_w), f32),
        ],
        scratch_shapes=[
            pltpu.VMEM((rows + SUBLANES, conv_w), f32),
            pltpu.VMEM((rows, conv_w), f32),
            pltpu.VMEM((rows, d_inner), f32),
            pltpu.VMEM((N_M, d_inner), f32),
            pltpu.VMEM((SSD_BLOCK, d_inner // G_M), f32),
        ],
        compiler_params=pltpu.CompilerParams(
            dimension_semantics=("parallel", "arbitrary"), vmem_limit_bytes=VMEM_LIMIT),
        name="ssd_prompt",
    )(x, g, *ws, cw, cb, dtb, aneg, dexp, nw, e_mat)


def _gdn_kernel(x_ref, g_ref, wx_ref, wgate_ref, wba_ref, cw_ref, dtb_ref, aneg_ref, nw_ref,
                o_ref, st_out_ref, cs_out_ref, ext_ref, act_ref, gate_ref, st_ref):
    i = pl.program_id(1)
    rows = x_ref.shape[0]
    blk = GDN_BLOCK
    ck = GDN_CHUNK
    conv_w = act_ref.shape[1]
    n_heads = st_ref.shape[0]
    hk = n_heads * DK
    per_blk = blk // ck
    heads = range(n_heads)

    @pl.when(i == 0)
    def _():
        ext_ref[0:SUBLANES, :] = jnp.zeros((SUBLANES, conv_w), f32)
        st_ref[...] = jnp.zeros(st_ref.shape, f32)

    h_in = _rms(x_ref[...], g_ref[...]).astype(bf16)
    _project_conv(h_in, wx_ref, ext_ref, act_ref, cw_ref, None, rows, conv_w)
    ba = _mm(h_in, wba_ref[...])
    gate_ref[...] = _mm(h_in, wgate_ref[...])

    r, cc = _iota2((blk, blk))
    same = (r // ck) == (cc // ck)
    lower = same & (r >= cc)
    strict = same & (r > cc)
    lower_b = jnp.where(lower, 1.0, 0.0).astype(bf16)
    upper3 = jnp.concatenate([jnp.where(same & (r <= cc), 1.0, 0.0).astype(bf16)] * 3, axis=0)
    rcol = r[:, 0:1]
    eye = jnp.where(r == cc, 1.0, 0.0)
    blk_masks = []
    size = GDN_BASE
    while size <= ck:
        blk_masks.append((r // size) == (cc // size))
        size *= 2

    n_sub = rows // blk
    prep = [None] * n_sub
    xs = [None] * n_sub
    qks, qgs, kgs, gls = ([None] * (n_sub * n_heads) for _ in range(4))

    def prepare(sub):
        rs = slice(sub * blk, (sub + 1) * blk)
        beta = jax.nn.sigmoid(ba[rs, :LANES])
        glog = aneg_ref[...] * _softplus(ba[rs, LANES:] + dtb_ref[...])
        gcum = _mm_sel_lhs(lower_b, glog)
        gcum_t = _mm_sel_rhs(glog.T, upper3)
        nmats, rhs = [], []
        for h in heads:
            q = act_ref[rs, h * DK:(h + 1) * DK]
            k = act_ref[rs, hk + h * DK:hk + (h + 1) * DK]
            v = act_ref[rs, 2 * hk + h * DV:2 * hk + (h + 1) * DV]
            qn = q * lax.rsqrt(jnp.sum(q * q, axis=-1, keepdims=True) + EPS) * (DK ** -0.5)
            kn = k * lax.rsqrt(jnp.sum(k * k, axis=-1, keepdims=True) + EPS)
            bcol = beta[:, h:h + 1]
            gcol = gcum[:, h:h + 1]
            eg = jnp.exp(gcol)
            dm = jnp.exp(jnp.where(lower, gcol - gcum_t[h:h + 1, :], -jnp.inf))
            knb = kn.astype(bf16)
            kb = kn * bcol
            nmats.append(jnp.where(strict, -(_nt(kb.astype(bf16), knb) * dm), 0.0))
            rhs.append(jnp.concatenate([v * bcol, kb * eg], axis=-1))
            glast = [gcol[(j + 1) * ck - 1:(j + 1) * ck] for j in range(per_blk)]
            glast_row = glast[per_blk - 1]
            for j in range(per_blk - 2, -1, -1):
                glast_row = jnp.where(rcol < (j + 1) * ck, glast[j], glast_row)
            idx = sub * n_heads + h
            qks[idx] = jnp.where(lower, _nt(qn.astype(bf16), knb) * dm, 0.0).astype(bf16)
            qgs[idx] = (qn * eg).astype(bf16)
            kgs[idx] = (kn * jnp.exp(glast_row - gcol)).astype(bf16)
            gls[idx] = [jnp.exp(gl) for gl in glast]
            yield
        prep[sub] = (nmats, rhs)

    def solve(sub):
        nmats, rhs = prep[sub]
        qs = [jnp.where(blk_masks[0], n, 0.0) for n in nmats]
        minv = [eye + q for q in qs]
        span = 2
        while span < GDN_BASE:
            qbs = [q.astype(bf16) for q in qs]
            qs = [_mm(qb, qb) for qb in qbs]
            yield
            minv = [m + _mm(m.astype(bf16), q.astype(bf16)) for m, q in zip(minv, qs)]
            yield
            span *= 2
        for lvl in range(1, len(blk_masks)):
            off = blk_masks[lvl] & jnp.logical_not(blk_masks[lvl - 1])
            mbs = [m.astype(bf16) for m in minv]
            ems = [_mm(jnp.where(off, n, 0.0).astype(bf16), mb) for n, mb in zip(nmats, mbs)]
            yield
            minv = [m + _mm(mb, em.astype(bf16)) for m, mb, em in zip(minv, mbs, ems)]
            yield
        xs[sub] = [_mm(m.astype(bf16), x.astype(bf16)) for m, x in zip(minv, rhs)]

    state = {"s": [st_ref[h] for h in heads]}
    zeros = jnp.zeros((ck, DV), bf16)

    def recur(sub):
        for j in range(per_blk):
            rl = slice(j * ck, (j + 1) * ck)
            ro = slice(sub * blk + j * ck, sub * blk + (j + 1) * ck)
            at = lambda lst, h: lst[sub * n_heads + h]
            ss = state["s"]
            sbs = [s.astype(bf16) for s in ss]
            vns = [xs[sub][h][rl, :DV] - _mm(xs[sub][h][rl, DV:].astype(bf16), sbs[h]) for h in heads]
            yield
            vfull = [jnp.concatenate([zeros] * j + [vns[h].astype(bf16)] + [zeros] * (per_blk - 1 - j), axis=0)
                     for h in heads]
            outs = [_mm(at(qgs, h)[rl], sbs[h]) + _mm(at(qks, h)[rl, :], vfull[h]) for h in heads]
            yield
            state["s"] = [ss[h] * at(gls, h)[j] + _tn(at(kgs, h)[rl], vns[h].astype(bf16)) for h in heads]
            yield
            for h in heads:
                gt = gate_ref[ro, h * DV:(h + 1) * DV]
                o_ref[ro, h * DV:(h + 1) * DV] = (_rms(outs[h], nw_ref[...]) * _silu(gt)).astype(o_ref.dtype)
            yield

    def emit(*gens):
        live = list(gens)
        while live:
            for gen in list(live):
                if next(gen, StopIteration) is StopIteration:
                    live.remove(gen)

    emit(prepare(0))
    for sub in range(n_sub):
        later = [prepare(sub + 1)] if sub + 1 < n_sub else []
        earlier = [recur(sub - 1)] if sub > 0 else []
        emit(solve(sub), *later, *earlier)
    emit(recur(n_sub - 1))
    for h in heads:
        st_ref[h] = state["s"][h]

    @pl.when(i == pl.num_programs(1) - 1)
    def _():
        cs_out_ref[0] = ext_ref[0:SUBLANES, :]
        st_out_ref[0] = st_ref[...]


def _gdn_prompt(x, g, ws, bsz, seq, cw, dtb, aneg, nw, n_heads):
    d_model = x.shape[1]
    conv_w = cw.shape[1]
    hv = n_heads * DV
    rows = STEP_TOKENS
    nb = seq // rows
    tok = lambda b, i: (b * nb + i, 0)
    res = lambda a: pl.BlockSpec(a.shape, lambda b, i: (0, 0), pipeline_mode=pl.Buffered(1))
    return pl.pallas_call(
        _gdn_kernel,
        grid=(bsz, nb),
        in_specs=[pl.BlockSpec((rows, d_model), tok)] + [res(a) for a in (g, *ws, cw, dtb, aneg, nw)],
        out_specs=[
            pl.BlockSpec((rows, hv), tok),
            pl.BlockSpec((1, n_heads, DK, DV), lambda b, i: (b, 0, 0, 0)),
            pl.BlockSpec((1, SUBLANES, conv_w), lambda b, i: (b, 0, 0)),
        ],
        out_shape=[
            jax.ShapeDtypeStruct((bsz * seq, hv), bf16),
            jax.ShapeDtypeStruct((bsz, n_heads, DK, DV), f32),
            jax.ShapeDtypeStruct((bsz, SUBLANES, conv_w), f32),
        ],
        scratch_shapes=[
            pltpu.VMEM((rows + SUBLANES, conv_w), f32),
            pltpu.VMEM((rows, conv_w), f32),
            pltpu.VMEM((rows, hv), f32),
            pltpu.VMEM((n_heads, DK, DV), f32),
        ],
        compiler_params=pltpu.CompilerParams(
            dimension_semantics=("parallel", "arbitrary"), vmem_limit_bytes=VMEM_LIMIT),
        name="gdn_prompt",
    )(x, g, *ws, cw, dtb, aneg, nw)


def _conv_step(u, cs_ref, cs_out_ref, cw_ref, cb_ref):
    cw = cw_ref[...]
    conv = u * cw[3:4]
    for j in range(CONV_K - 1):
        conv = conv + cs_ref[j] * cw[j:j + 1]
    if cb_ref is not None:
        conv = conv + cb_ref[...]
    for j in range(CONV_K - 2):
        cs_out_ref[j] = cs_ref[j + 1]
    cs_out_ref[CONV_K - 2] = u
    return _silu(conv)


def _ssd_step_kernel(xbc_ref, z_ref, dt_ref, cs_ref, st_ref, cw_ref, cb_ref, dtb_ref, aneg_ref, dexp_ref,
                     nw_ref, e_ref, y_ref, cs_out_ref, st_out_ref, yacc_ref):
    bb = xbc_ref.shape[0]
    d_inner = y_ref.shape[1]
    gw = d_inner // G_M
    act = _conv_step(xbc_ref[...], cs_ref, cs_out_ref, cw_ref, cb_ref)
    z = z_ref[...]
    dt = _softplus(dt_ref[...] + dtb_ref[...])
    dec = jnp.exp(dt * aneg_ref[...])
    se = _mm(_pack3_lanes(jnp.concatenate([dt, dec], axis=0), d_inner // P_M), e_ref[...])
    xs = act[:, :d_inner]
    xdt = xs * se[0:bb]
    dec_e = se[bb:2 * bb]
    d3 = jnp.concatenate(_split3(dec_e), axis=0)
    r3, c3 = _iota2((3 * bb, bb * N_M))
    seq3 = c3 // N_M
    onehot3 = jnp.where((r3 == seq3) | (r3 == seq3 + bb) | (r3 == seq3 + 2 * bb), 1.0, 0.0).astype(bf16)
    rb, cb_i = _iota2((bb, bb * N_M))
    own = rb == cb_i // N_M
    rowid = rb[:, 0:1]

    for g in range(G_M):
        gs = slice(g * gw, (g + 1) * gw)
        bm = act[:, d_inner + g * N_M:d_inner + (g + 1) * N_M]
        cm = act[:, d_inner + (G_M + g) * N_M:d_inner + (G_M + g + 1) * N_M].astype(bf16)
        bdiag = jnp.where(own, jnp.concatenate([bm] * bb, axis=1), 0.0).astype(bf16)
        dcol = _tn(d3[:, gs], onehot3)
        outer = _tn(xdt[:, gs].astype(bf16), bdiag)
        yg = None
        for b in range(bb):
            bs = slice(b * N_M, (b + 1) * N_M)
            s_new = st_ref[b, gs, :] * dcol[:, bs] + outer[:, bs]
            st_out_ref[b, gs, :] = s_new
            yb = _nt(cm, s_new.astype(bf16))
            yg = yb if yg is None else jnp.where(rowid == b, yb, yg)
        yacc_ref[:, gs] = yg

    y = yacc_ref[...] + xs * dexp_ref[...]
    y = y * _silu(z)
    for g in range(G_M):
        gs = slice(g * gw, (g + 1) * gw)
        y_ref[:, gs] = _rms(y[:, gs], nw_ref[:, gs]).astype(y_ref.dtype)


def _ssd_sample(projs, conv_state, state, cw, cb, dtb, aneg, dexp, nw, e_mat):
    bsz = projs[0].shape[0]
    d_inner = dexp.shape[1]
    conv_w = cw.shape[1]
    bb = SAMPLE_BB
    const = lambda a: pl.BlockSpec(a.shape, lambda i: (0, 0))
    return pl.pallas_call(
        _ssd_step_kernel,
        grid=(bsz // bb,),
        in_specs=[pl.BlockSpec((bb, p.shape[1]), lambda i: (i, 0)) for p in projs] + [
            pl.BlockSpec((CONV_K - 1, bb, conv_w), lambda i: (0, i, 0)),
            pl.BlockSpec((bb, d_inner, N_M), lambda i: (i, 0, 0)),
        ] + [const(a) for a in (cw, cb, dtb, aneg, dexp, nw, e_mat)],
        out_specs=[
            pl.BlockSpec((bb, d_inner), lambda i: (i, 0)),
            pl.BlockSpec((CONV_K - 1, bb, conv_w), lambda i: (0, i, 0)),
            pl.BlockSpec((bb, d_inner, N_M), lambda i: (i, 0, 0)),
        ],
        out_shape=[
            jax.ShapeDtypeStruct((bsz, d_inner), bf16),
            jax.ShapeDtypeStruct((CONV_K - 1, bsz, conv_w), f32),
            jax.ShapeDtypeStruct((bsz, d_inner, N_M), f32),
        ],
        scratch_shapes=[pltpu.VMEM((bb, d_inner), f32)],
        compiler_params=pltpu.CompilerParams(
            dimension_semantics=("parallel",), vmem_limit_bytes=VMEM_LIMIT),
        name="ssd_sample",
    )(*projs, conv_state, state, cw, cb, dtb, aneg, dexp, nw, e_mat)


def _gdn_step_kernel(qkv_ref, gate_ref, ba_ref, cs_ref, st_ref, cw_ref, dtb_ref, aneg_ref, nw_ref,
                     o_ref, cs_out_ref, st_out_ref):
    bb = qkv_ref.shape[0]
    n_heads = st_ref.shape[1]
    hk = n_heads * DK
    act = _conv_step(qkv_ref[...], cs_ref, cs_out_ref, cw_ref, None)
    beta = jax.nn.sigmoid(ba_ref[:, :LANES])
    eg = jnp.exp(aneg_ref[...] * _softplus(ba_ref[:, LANES:] + dtb_ref[...]))
    rb, cb_i = _iota2((bb, bb * DV))
    own = rb == cb_i // DV
    rowid = rb[:, 0:1]

    for h in range(n_heads):
        q = act[:, h * DK:(h + 1) * DK]
        k = act[:, hk + h * DK:hk + (h + 1) * DK]
        v = act[:, 2 * hk + h * DV:2 * hk + (h + 1) * DV]
        qn = q * lax.rsqrt(jnp.sum(q * q, axis=-1, keepdims=True) + EPS) * (DK ** -0.5)
        kn = k * lax.rsqrt(jnp.sum(k * k, axis=-1, keepdims=True) + EPS)
        bcol = beta[:, h:h + 1]
        ecol = eg[:, h:h + 1]
        kb = kn * bcol
        u = v * bcol
        qk = jnp.sum(qn.astype(bf16).astype(f32) * kn.astype(bf16).astype(f32), axis=-1, keepdims=True)
        wq = jnp.concatenate([(kb * ecol).astype(bf16), (qn * ecol).astype(bf16)], axis=0)
        vn = o = None
        for b in range(bb):
            ws = _mm(wq, st_ref[b, h].astype(bf16))
            vn_b = u - ws[:bb]
            o_b = ws[bb:] + qk * vn_b
            vn = vn_b if vn is None else jnp.where(rowid == b, vn_b, vn)
            o = o_b if o is None else jnp.where(rowid == b, o_b, o)
        vdiag = jnp.where(own, jnp.concatenate([vn] * bb, axis=1), 0.0).astype(bf16)
        outer = _tn(kn.astype(bf16), vdiag)
        for b in range(bb):
            st_out_ref[b, h] = st_ref[b, h] * ecol[b:b + 1, :] + outer[:, b * DV:(b + 1) * DV]
        gt = gate_ref[:, h * DV:(h + 1) * DV]
        o_ref[:, h * DV:(h + 1) * DV] = (_rms(o, nw_ref[...]) * _silu(gt)).astype(o_ref.dtype)


def _gdn_sample(projs, conv_state, state, cw, dtb, aneg, nw):
    bsz = projs[0].shape[0]
    n_heads = state.shape[1]
    conv_w = cw.shape[1]
    hv = n_heads * DV
    bb = SAMPLE_BB_GDN
    const = lambda a: pl.BlockSpec(a.shape, lambda i: (0, 0))
    return pl.pallas_call(
        _gdn_step_kernel,
        grid=(bsz // bb,),
        in_specs=[pl.BlockSpec((bb, p.shape[1]), lambda i: (i, 0)) for p in projs] + [
            pl.BlockSpec((CONV_K - 1, bb, conv_w), lambda i: (0, i, 0)),
            pl.BlockSpec((bb, n_heads, DK, DV), lambda i: (i, 0, 0, 0)),
        ] + [const(a) for a in (cw, dtb, aneg, nw)],
        out_specs=[
            pl.BlockSpec((bb, hv), lambda i: (i, 0)),
            pl.BlockSpec((CONV_K - 1, bb, conv_w), lambda i: (0, i, 0)),
            pl.BlockSpec((bb, n_heads, DK, DV), lambda i: (i, 0, 0, 0)),
        ],
        out_shape=[
            jax.ShapeDtypeStruct((bsz, hv), bf16),
            jax.ShapeDtypeStruct((CONV_K - 1, bsz, conv_w), f32),
            jax.ShapeDtypeStruct((bsz, n_heads, DK, DV), f32),
        ],
        compiler_params=pltpu.CompilerParams(
            dimension_semantics=("parallel",), vmem_limit_bytes=VMEM_LIMIT),
        name="gdn_sample",
    )(*projs, conv_state, state, cw, dtb, aneg, nw)


def _tail_kernel(x_ref, y_ref, o_ref, p_ref, nmix_ref, wmg_ref, wbs_ref, wbg_ref, wout_ref, nf_ref, wfi_ref,
                 wfo_ref, npl_ref, wpg_ref, wpp_ref, nfin_ref, out_ref, *, ff_chunks, final_norm):
    d = x_ref.shape[1]
    d_ff = wfo_ref.shape[0]
    x = x_ref[...]
    mg = _mm(_rms(x, nmix_ref[...]).astype(bf16), wmg_ref[...])
    mix = (jax.nn.sigmoid(mg[:, :d]) * _mm(y_ref[...], wbs_ref[...])
           + jax.nn.sigmoid(mg[:, d:]) * _mm(o_ref[...], wbg_ref[...]))
    x = x + _mm(mix.astype(bf16), wout_ref[...])
    h = _rms(x, nf_ref[...]).astype(bf16)
    fc = d_ff // ff_chunks
    for c in range(ff_chunks):
        gt = _mm(h, wfi_ref[:, c * fc:(c + 1) * fc])
        up = _mm(h, wfi_ref[:, d_ff + c * fc:d_ff + (c + 1) * fc])
        x = x + _mm((_silu(gt) * up).astype(bf16), wfo_ref[c * fc:(c + 1) * fc, :])
    pe = _mm(p_ref[...].astype(bf16), wpp_ref[...])
    x = x + pe * jax.nn.sigmoid(_mm(_rms(x, npl_ref[...]).astype(bf16), wpg_ref[...]))
    if final_norm:
        x = _rms(x, nfin_ref[...])
    out_ref[...] = x


def _tail(x, y, o, p, wts, tm, final_norm):
    t, d = x.shape
    tok = lambda w: pl.BlockSpec((tm, w), lambda i: (i, 0))
    res = lambda a: pl.BlockSpec(a.shape, lambda i: (0, 0), pipeline_mode=pl.Buffered(1))
    names = ("nmix", "wmg", "wbs", "wbg", "wout", "nf", "wfi", "wfo", "npl", "wpg", "wpp", "nfin")
    return pl.pallas_call(
        functools.partial(_tail_kernel, ff_chunks=2, final_norm=final_norm),
        grid=(t // tm,),
        in_specs=[tok(d), tok(y.shape[1]), tok(o.shape[1]), tok(p.shape[1])] + [res(wts[n]) for n in names],
        out_specs=tok(d),
        out_shape=jax.ShapeDtypeStruct((t, d), f32),
        compiler_params=pltpu.CompilerParams(
            dimension_semantics=("parallel",), vmem_limit_bytes=VMEM_LIMIT),
        name="tail",
    )(x, y, o, p, *[wts[n] for n in names])


def _pad_lanes(v, width=LANES):
    return jnp.pad(v, ((0, 0), (0, width - v.shape[1])))


def _layer_weights(i, d_model, norm_mix, w_in, ssm_conv_w, ssm_conv_b, ssm_dt_bias, ssm_a_log, ssm_d, ssm_norm,
                   gdn_conv_w, gdn_dt_bias, gdn_a_log, gdn_norm, w_branch_ssm, w_branch_gdn, w_out,
                   norm_ffn, w_ffn_in, w_ffn_out, norm_pl, w_pl_gate, w_pl_proj, norm_final):
    h_m = ssm_dt_bias.shape[1]
    h_g = gdn_dt_bias.shape[1]
    d_inner = h_m * P_M
    conv_m = ssm_conv_w.shape[2]
    conv_g = gdn_conv_w.shape[2]
    sizes = (d_inner, conv_m, h_m, conv_g, h_g * DV, h_g, h_g, 2 * d_model)
    starts = [0]
    for s in sizes:
        starts.append(starts[-1] + s)
    seg = lambda j: w_in[i][:, starts[j]:starts[j + 1]]
    z, xbc, dtw, qkv, gate, bw, aw, mg = (seg(j) for j in range(8))
    w_ssd = (xbc.astype(bf16), z.astype(bf16), _pad_lanes(dtw).astype(bf16))
    w_gdn = (qkv.astype(bf16), gate.astype(bf16),
             jnp.concatenate([_pad_lanes(bw), _pad_lanes(aw)], axis=1).astype(bf16))
    head_of = jnp.arange(d_inner, dtype=jnp.int32) // P_M
    assert 3 * h_m <= LANES
    rows_e = jnp.arange(LANES, dtype=jnp.int32)[:, None]
    e_mat = ((rows_e % h_m == head_of[None, :]) & (rows_e < 3 * h_m)).astype(bf16)
    row = lambda v: v.reshape(1, -1).astype(f32)
    return dict(
        norm_mix=row(norm_mix[i]), w_ssd=w_ssd, w_gdn=w_gdn,
        ssm=dict(cw=ssm_conv_w[i], cb=row(ssm_conv_b[i]), dtb=_pad_lanes(row(ssm_dt_bias[i])),
                 aneg=_pad_lanes(-jnp.exp(row(ssm_a_log[i]))), dexp=row(jnp.repeat(ssm_d[i], P_M)),
                 nw=row(ssm_norm[i]), e_mat=e_mat),
        gdn=dict(cw=gdn_conv_w[i], dtb=_pad_lanes(row(gdn_dt_bias[i])),
                 aneg=_pad_lanes(-jnp.exp(row(gdn_a_log[i]))), nw=row(gdn_norm[i])),
        tail=dict(nmix=row(norm_mix[i]), wmg=mg.astype(bf16),
                  wbs=w_branch_ssm[i].astype(bf16), wbg=w_branch_gdn[i].astype(bf16), wout=w_out[i].astype(bf16),
                  nf=row(norm_ffn[i]), wfi=w_ffn_in[i].astype(bf16), wfo=w_ffn_out[i].astype(bf16),
                  npl=row(norm_pl[i]), wpg=w_pl_gate[i].astype(bf16), wpp=w_pl_proj[i].astype(bf16),
                  nfin=row(norm_final)),
    )


def _pick_tile(n, candidates):
    for c in candidates:
        if n % c == 0:
            return c
    raise ValueError(f"no tile for {n}")


def kernel(x_prompt, x_sample, p_prompt, p_sample, state_ssm, state_ssm_conv, state_gdn, state_gdn_conv,
           norm_mix, w_in, ssm_conv_w, ssm_conv_b, ssm_dt_bias, ssm_a_log, ssm_d, ssm_norm,
           gdn_conv_w, gdn_dt_bias, gdn_a_log, gdn_norm, w_branch_ssm, w_branch_gdn, w_out,
           norm_ffn, w_ffn_in, w_ffn_out, norm_pl, w_pl_gate, w_pl_proj, norm_final):
    depth = p_prompt.shape[0]
    bp, seq, d_model = x_prompt.shape
    bs, dec_seq, _ = x_sample.shape
    assert dec_seq == 1 and seq % STEP_TOKENS == 0 and bs % SAMPLE_BB == 0 and bs % SAMPLE_BB_GDN == 0
    h_m, h_g = ssm_dt_bias.shape[1], gdn_dt_bias.shape[1]
    d_inner = h_m * P_M
    tail_rows = slice(SUBLANES - (CONV_K - 1), SUBLANES)

    xp = x_prompt.reshape(bp * seq, d_model)
    xs = x_sample.reshape(bs, d_model)
    new_p = ([], [], [], [])
    new_s = ([], [], [], [])
    for i in range(depth):
        lw = _layer_weights(i, d_model, norm_mix, w_in, ssm_conv_w, ssm_conv_b, ssm_dt_bias, ssm_a_log, ssm_d,
                            ssm_norm, gdn_conv_w, gdn_dt_bias, gdn_a_log, gdn_norm, w_branch_ssm, w_branch_gdn,
                            w_out, norm_ffn, w_ffn_in, w_ffn_out, norm_pl, w_pl_gate, w_pl_proj, norm_final)
        last = i == depth - 1

        tp = bp * seq
        y, st_ssm, cs_ssm = _ssd_prompt(xp, lw["norm_mix"], lw["w_ssd"], bp, seq, **lw["ssm"])
        o, st_gdn, cs_gdn = _gdn_prompt(xp, lw["norm_mix"], lw["w_gdn"], bp, seq, n_heads=h_g, **lw["gdn"])
        xp = _tail(xp, y, o, p_prompt[i].reshape(tp, -1), lw["tail"], _pick_tile(tp, (256, 128)), last)
        new_p[0].append(st_ssm.reshape(bp, h_m, P_M, N_M))
        new_p[1].append(cs_ssm[:, tail_rows, :])
        new_p[2].append(st_gdn)
        new_p[3].append(cs_gdn[:, tail_rows, :])

        ys, cs_s, ss_s = _ssd_sample(_inproj(xs, lw["norm_mix"], lw["w_ssd"]),
                                     jnp.swapaxes(state_ssm_conv[i], 0, 1),
                                     state_ssm[i].reshape(bs, d_inner, N_M), **lw["ssm"])
        os_, cs_g, ss_g = _gdn_sample(_inproj(xs, lw["norm_mix"], lw["w_gdn"]),
                                      jnp.swapaxes(state_gdn_conv[i], 0, 1), state_gdn[i], **lw["gdn"])
        xs = _tail(xs, ys, os_, p_sample[i].reshape(bs, -1), lw["tail"],
                   _pick_tile(bs, (128, 64, 32, 16, 8)), last)
        new_s[0].append(ss_s.reshape(bs, h_m, P_M, N_M))
        new_s[1].append(jnp.swapaxes(cs_s, 0, 1))
        new_s[2].append(ss_g)
        new_s[3].append(jnp.swapaxes(cs_g, 0, 1))

    stack = lambda lst: jnp.stack(lst)
    return (xp.reshape(bp, seq, d_model), xs.reshape(bs, 1, d_model),
            stack(new_p[0]), stack(new_p[1]), stack(new_p[2]), stack(new_p[3]),
            stack(new_s[0]), stack(new_s[1]), stack(new_s[2]), stack(new_s[3]))
```

```python
import functools

import jax
import jax.numpy as jnp
from jax import lax
from jax.experimental import pallas as pl
from jax.experimental.pallas import tpu as pltpu

f32 = jnp.float32
bf16 = jnp.bfloat16

EPS = 1e-6
CONV_K = 4
LANES = 128
SUBLANES = 8
VMEM_LIMIT = 56 * 1024 * 1024

P_M = 64
N_M = 128
G_M = 4
DK = 128
DV = 128
SSD_BLOCK = 128
GDN_CHUNK = 64
GDN_BLOCK = 128
GDN_BASE = 16
STEP_TOKENS = 512
CONV_COLS = 512
SAMPLE_BB = 8
SAMPLE_BB_GDN = 16


def _nt(a, b):
    return lax.dot_general(a, b, (((1,), (1,)), ((), ())), preferred_element_type=f32)


def _tn(a, b):
    return lax.dot_general(a, b, (((0,), (0,)), ((), ())), preferred_element_type=f32)


def _mm(a, b):
    return jnp.dot(a, b, preferred_element_type=f32)


def _split3(x):
    hi = x.astype(bf16)
    r1 = x - hi.astype(f32)
    mid = r1.astype(bf16)
    lo = (r1 - mid.astype(f32)).astype(bf16)
    return hi, mid, lo


def _mm_sel_rhs(x, sel3):
    return _mm(jnp.concatenate(_split3(x), axis=1), sel3)


def _mm_sel_lhs(sel, x):
    return _mm(jnp.concatenate([sel] * 3, axis=1), jnp.concatenate(_split3(x), axis=0))


def _pack3_lanes(x, width):
    lane = lax.broadcasted_iota(jnp.int32, x.shape, 1)
    hi, mid, lo = (t.astype(f32) for t in _split3(jnp.where(lane < width, x, 0.0)))
    return (hi + pltpu.roll(mid, width, 1) + pltpu.roll(lo, 2 * width, 1)).astype(bf16)


def _silu(x):
    h = 0.5 * x
    return h + h * jnp.tanh(h)


def _softplus(x):
    return jnp.maximum(x, 0.0) + jnp.log1p(jnp.exp(-jnp.abs(x)))


def _rms(x, g):
    return x * lax.rsqrt(jnp.mean(x * x, axis=-1, keepdims=True) + EPS) * g


def _iota2(shape):
    return lax.broadcasted_iota(jnp.int32, shape, 0), lax.broadcasted_iota(jnp.int32, shape, 1)


def _inproj_kernel(x_ref, g_ref, *refs):
    n = len(refs) // 2
    h = _rms(x_ref[...], g_ref[...]).astype(bf16)
    for w_ref, o_ref in zip(refs[:n], refs[n:]):
        o_ref[...] = _nt(h, w_ref[...])


def _inproj(x, g, ws):
    t, d = x.shape
    full = lambda shape: pl.BlockSpec(shape, lambda i: (0, 0))
    return pl.pallas_call(
        _inproj_kernel,
        grid=(1,),
        in_specs=[full((t, d)), full((1, d))] + [full(w.shape) for w in ws],
        out_specs=[full((t, w.shape[0])) for w in ws],
        out_shape=[jax.ShapeDtypeStruct((t, w.shape[0]), f32) for w in ws],
        compiler_params=pltpu.CompilerParams(dimension_semantics=("arbitrary",), vmem_limit_bytes=VMEM_LIMIT),
        name="inproj",
    )(x, g, *ws)


def _project_conv(h, w_ref, ext_ref, act_ref, cw_ref, cb_ref, rows, width):
    slabs = [slice(c0, c0 + CONV_COLS) for c0 in range(0, width, CONV_COLS)]
    u_next = _nt(h, w_ref[slabs[0], :])
    for k, sl in enumerate(slabs):
        u = u_next
        if k + 1 < len(slabs):
            u_next = _nt(h, w_ref[slabs[k + 1], :])
        ext_ref[SUBLANES:SUBLANES + rows, sl] = u
        cw = cw_ref[:, sl]
        conv = u * cw[3:4]
        full = ext_ref[:, sl]
        for j in range(CONV_K - 1):
            shifted = pltpu.roll(full, CONV_K - 1 - j, 0)
            conv = conv + shifted[SUBLANES:SUBLANES + rows] * cw[j:j + 1]
        if cb_ref is not None:
            conv = conv + cb_ref[:, sl]
        act_ref[:, sl] = _silu(conv)
        ext_ref[0:SUBLANES, sl] = ext_ref[rows:rows + SUBLANES, sl]


def _ssd_kernel(x_ref, g_ref, wx_ref, wz_ref, wdt_ref, cw_ref, cb_ref, dtb_ref, aneg_ref, dexp_ref, nw_ref, e_ref,
                y_ref, st_out_ref, cs_out_ref, ext_ref, act_ref, z_ref, st_ref, yd_ref):
    i = pl.program_id(1)
    rows = x_ref.shape[0]
    c = SSD_BLOCK
    conv_w = act_ref.shape[1]
    d_inner = y_ref.shape[1]
    gw = d_inner // G_M
    hg = gw // P_M
    n_heads = d_inner // P_M

    @pl.when(i == 0)
    def _():
        ext_ref[0:SUBLANES, :] = jnp.zeros((SUBLANES, conv_w), f32)
        st_ref[...] = jnp.zeros(st_ref.shape, f32)

    h = _rms(x_ref[...], g_ref[...]).astype(bf16)
    _project_conv(h, wx_ref, ext_ref, act_ref, cw_ref, cb_ref, rows, conv_w)
    dt_raw = _nt(h, wdt_ref[...])
    for g in range(G_M):
        z_ref[:, g * gw:(g + 1) * gw] = _nt(h, wz_ref[g * gw:(g + 1) * gw, :])

    r, cc = _iota2((c, c))
    lower = r >= cc
    lower_b = jnp.where(lower, 1.0, 0.0).astype(bf16)
    upper3 = jnp.concatenate([jnp.where(r <= cc, 1.0, 0.0).astype(bf16)] * 3, axis=0)

    for sub in range(rows // c):
        rs = slice(sub * c, (sub + 1) * c)
        dt = _softplus(dt_raw[rs] + dtb_ref[...])
        da = dt * aneg_ref[...]
        acum = _mm_sel_lhs(lower_b, da)
        acum_t = _mm_sel_rhs(da.T, upper3)
        alast = acum[c - 1:c, :]
        dout = jnp.exp(alast - acum)
        scal = _pack3_lanes(jnp.concatenate(
            [dt, dt * dout, jnp.exp(acum), jnp.broadcast_to(jnp.exp(alast), (SUBLANES, LANES))], axis=0), n_heads)
        for g in range(G_M):
            gs = slice(g * gw, (g + 1) * gw)
            se = _mm(scal, e_ref[:, gs])
            dt_e, dd_e, ea_e, cd_e = se[0:c], se[c:2 * c], se[2 * c:3 * c], se[3 * c:3 * c + 1]
            xs = act_ref[rs, gs]
            bm = act_ref[rs, d_inner + g * N_M:d_inner + (g + 1) * N_M].astype(bf16)
            cm = act_ref[rs, d_inner + (G_M + g) * N_M:d_inner + (G_M + g + 1) * N_M].astype(bf16)
            xdt = (xs * dt_e).astype(bf16)
            cb = _nt(cm, bm)
            st = st_ref[:, gs]
            y_off = _mm(cm, st.astype(bf16)) * ea_e
            st_ref[:, gs] = st * cd_e + _tn(bm, (xs * dd_e).astype(bf16))
            for hh in range(hg):
                hd = g * hg + hh
                lmat = jnp.exp(jnp.where(lower, acum[:, hd:hd + 1] - acum_t[hd:hd + 1, :], -jnp.inf))
                yd_ref[:, hh * P_M:(hh + 1) * P_M] = _mm((cb * lmat).astype(bf16), xdt[:, hh * P_M:(hh + 1) * P_M])
            y = yd_ref[...] + y_off + xs * dexp_ref[:, gs]
            y = y * _silu(z_ref[rs, gs])
            y_ref[rs, gs] = _rms(y, nw_ref[:, gs]).astype(y_ref.dtype)

    @pl.when(i == pl.num_programs(1) - 1)
    def _():
        cs_out_ref[0] = ext_ref[0:SUBLANES, :]
        for j in range(d_inner // LANES):
            st_out_ref[0, j * LANES:(j + 1) * LANES, :] = st_ref[:, j * LANES:(j + 1) * LANES].T


def _ssd_prompt(x, g, ws, bsz, seq, cw, cb, dtb, aneg, dexp, nw, e_mat):
    d_model = x.shape[1]
    d_inner = dexp.shape[1]
    conv_w = cw.shape[1]
    rows = STEP_TOKENS
    nb = seq // rows
    tok = lambda b, i: (b * nb + i, 0)
    res = lambda a: pl.BlockSpec(a.shape, lambda b, i: (0, 0), pipeline_mode=pl.Buffered(1))
    return pl.pallas_call(
        _ssd_kernel,
        grid=(bsz, nb),
        in_specs=[pl.BlockSpec((rows, d_model), tok)] + [res(a) for a in (g, *ws, cw, cb, dtb, aneg, dexp, nw, e_mat)],
        out_specs=[
            pl.BlockSpec((rows, d_inner), tok),
            pl.BlockSpec((1, d_inner, N_M), lambda b, i: (b, 0, 0)),
            pl.BlockSpec((1, SUBLANES, conv_w), lambda b, i: (b, 0, 0)),
        ],
        out_shape=[
            jax.ShapeDtypeStruct((bsz * seq, d_inner), bf16),
            jax.ShapeDtypeStruct((bsz, d_inner, N_M), f32),
            jax.ShapeDtypeStruct((bsz, SUBLANES, conv_w), f32),
        ],
        scratch_shapes=[
            pltpu.VMEM((rows + SUBLANES, conv_w), f32),
            pltpu.VMEM((rows, conv_w), f32),
            pltpu.VMEM((rows, d_inner), f32),
            pltpu.VMEM((N_M, d_inner), f32),
            pltpu.VMEM((SSD_BLOCK, d_inner // G_M), f32),
        ],
        compiler_params=pltpu.CompilerParams(
            dimension_semantics=("parallel", "arbitrary"), vmem_limit_bytes=VMEM_LIMIT),
        name="ssd_prompt",
    )(x, g, *ws, cw, cb, dtb, aneg, dexp, nw, e_mat)


def _gdn_kernel(x_ref, g_ref, wx_ref, wgate_ref, wba_ref, cw_ref, dtb_ref, aneg_ref, nw_ref,
                o_ref, st_out_ref, cs_out_ref, ext_ref, act_ref, gate_ref, st_ref):
    i = pl.program_id(1)
    rows = x_ref.shape[0]
    blk = GDN_BLOCK
    ck = GDN_CHUNK
    conv_w = act_ref.shape[1]
    n_heads = st_ref.shape[0]
    hk = n_heads * DK
    per_blk = blk // ck
    heads = range(n_heads)

    @pl.when(i == 0)
    def _():
        ext_ref[0:SUBLANES, :] = jnp.zeros((SUBLANES, conv_w), f32)
        st_ref[...] = jnp.zeros(st_ref.shape, f32)

    h_in = _rms(x_ref[...], g_ref[...]).astype(bf16)
    _project_conv(h_in, wx_ref, ext_ref, act_ref, cw_ref, None, rows, conv_w)
    ba = _nt(h_in, wba_ref[...])
    gate_ref[...] = _nt(h_in, wgate_ref[...])

    r, cc = _iota2((blk, blk))
    same = (r // ck) == (cc // ck)
    lower = same & (r >= cc)
    strict = same & (r > cc)
    lower_b = jnp.where(lower, 1.0, 0.0).astype(bf16)
    upper3 = jnp.concatenate([jnp.where(same & (r <= cc), 1.0, 0.0).astype(bf16)] * 3, axis=0)
    rcol = r[:, 0:1]
    eye = jnp.where(r == cc, 1.0, 0.0)
    blk_masks = []
    size = GDN_BASE
    while size <= ck:
        blk_masks.append((r // size) == (cc // size))
        size *= 2

    n_sub = rows // blk
    prep = [None] * n_sub
    xs = [None] * n_sub
    qks, qgs, kgs, gls = ([None] * (n_sub * n_heads) for _ in range(4))

    def prepare(sub):
        rs = slice(sub * blk, (sub + 1) * blk)
        beta = jax.nn.sigmoid(ba[rs, :LANES])
        glog = aneg_ref[...] * _softplus(ba[rs, LANES:] + dtb_ref[...])
        gcum = _mm_sel_lhs(lower_b, glog)
        gcum_t = _mm_sel_rhs(glog.T, upper3)
        nmats, rhs = [], []
        for h in heads:
            q = act_ref[rs, h * DK:(h + 1) * DK]
            k = act_ref[rs, hk + h * DK:hk + (h + 1) * DK]
            v = act_ref[rs, 2 * hk + h * DV:2 * hk + (h + 1) * DV]
            qn = q * lax.rsqrt(jnp.sum(q * q, axis=-1, keepdims=True) + EPS) * (DK ** -0.5)
            kn = k * lax.rsqrt(jnp.sum(k * k, axis=-1, keepdims=True) + EPS)
            bcol = beta[:, h:h + 1]
            gcol = gcum[:, h:h + 1]
            eg = jnp.exp(gcol)
            dm = jnp.exp(jnp.where(lower, gcol - gcum_t[h:h + 1, :], -jnp.inf))
            knb = kn.astype(bf16)
            kb = kn * bcol
            nmats.append(jnp.where(strict, -(_nt(kb.astype(bf16), knb) * dm), 0.0))
            rhs.append(jnp.concatenate([v * bcol, kb * eg], axis=-1))
            glast = [gcol[(j + 1) * ck - 1:(j + 1) * ck] for j in range(per_blk)]
            glast_row = glast[per_blk - 1]
            for j in range(per_blk - 2, -1, -1):
                glast_row = jnp.where(rcol < (j + 1) * ck, glast[j], glast_row)
            idx = sub * n_heads + h
            qks[idx] = jnp.where(lower, _nt(qn.astype(bf16), knb) * dm, 0.0).astype(bf16)
            qgs[idx] = (qn * eg).astype(bf16)
            kgs[idx] = (kn * jnp.exp(glast_row - gcol)).astype(bf16)
            gls[idx] = [jnp.exp(gl) for gl in glast]
            yield
        prep[sub] = (nmats, rhs)

    def solve(sub):
        nmats, rhs = prep[sub]
        qs = [jnp.where(blk_masks[0], n, 0.0) for n in nmats]
        minv = [eye + q for q in qs]
        span = 2
        while span < GDN_BASE:
            qbs = [q.astype(bf16) for q in qs]
            qs = [_mm(qb, qb) for qb in qbs]
            yield
            minv = [m + _mm(m.astype(bf16), q.astype(bf16)) for m, q in zip(minv, qs)]
            yield
            span *= 2
        for lvl in range(1, len(blk_masks)):
            off = blk_masks[lvl] & jnp.logical_not(blk_masks[lvl - 1])
            mbs = [m.astype(bf16) for m in minv]
            ems = [_mm(jnp.where(off, n, 0.0).astype(bf16), mb) for n, mb in zip(nmats, mbs)]
            yield
            minv = [m + _mm(mb, em.astype(bf16)) for m, mb, em in zip(minv, mbs, ems)]
            yield
        xs[sub] = [_mm(m.astype(bf16), x.astype(bf16)) for m, x in zip(minv, rhs)]

    state = {"s": [st_ref[h] for h in heads]}
    zeros = jnp.zeros((ck, DV), bf16)

    def recur(sub):
        for j in range(per_blk):
            rl = slice(j * ck, (j + 1) * ck)
            ro = slice(sub * blk + j * ck, sub * blk + (j + 1) * ck)
            at = lambda lst, h: lst[sub * n_heads + h]
            ss = state["s"]
            sbs = [s.astype(bf16) for s in ss]
            vns = [xs[sub][h][rl, :DV] - _mm(xs[sub][h][rl, DV:].astype(bf16), sbs[h]) for h in heads]
            yield
            vfull = [jnp.concatenate([zeros] * j + [vns[h].astype(bf16)] + [zeros] * (per_blk - 1 - j), axis=0)
                     for h in heads]
            outs = [_mm(at(qgs, h)[rl], sbs[h]) + _mm(at(qks, h)[rl, :], vfull[h]) for h in heads]
            yield
            state["s"] = [ss[h] * at(gls, h)[j] + _tn(at(kgs, h)[rl], vns[h].astype(bf16)) for h in heads]
            yield
            for h in heads:
                gt = gate_ref[ro, h * DV:(h + 1) * DV]
                o_ref[ro, h * DV:(h + 1) * DV] = (_rms(outs[h], nw_ref[...]) * _silu(gt)).astype(o_ref.dtype)
            yield

    def emit(*gens):
        live = list(gens)
        while live:
            for gen in list(live):
                if next(gen, StopIteration) is StopIteration:
                    live.remove(gen)

    emit(prepare(0))
    for sub in range(n_sub):
        later = [prepare(sub + 1)] if sub + 1 < n_sub else []
        earlier = [recur(sub - 1)] if sub > 0 else []
        emit(solve(sub), *later, *earlier)
    emit(recur(n_sub - 1))
    for h in heads:
        st_ref[h] = state["s"][h]

    @pl.when(i == pl.num_programs(1) - 1)
    def _():
        cs_out_ref[0] = ext_ref[0:SUBLANES, :]
        st_out_ref[0] = st_ref[...]


def _gdn_prompt(x, g, ws, bsz, seq, cw, dtb, aneg, nw, n_heads):
    d_model = x.shape[1]
    conv_w = cw.shape[1]
    hv = n_heads * DV
    rows = STEP_TOKENS
    nb = seq // rows
    tok = lambda b, i: (b * nb + i, 0)
    res = lambda a: pl.BlockSpec(a.shape, lambda b, i: (0, 0), pipeline_mode=pl.Buffered(1))
    return pl.pallas_call(
        _gdn_kernel,
        grid=(bsz, nb),
        in_specs=[pl.BlockSpec((rows, d_model), tok)] + [res(a) for a in (g, *ws, cw, dtb, aneg, nw)],
        out_specs=[
            pl.BlockSpec((rows, hv), tok),
            pl.BlockSpec((1, n_heads, DK, DV), lambda b, i: (b, 0, 0, 0)),
            pl.BlockSpec((1, SUBLANES, conv_w), lambda b, i: (b, 0, 0)),
        ],
        out_shape=[
            jax.ShapeDtypeStruct((bsz * seq, hv), bf16),
            jax.ShapeDtypeStruct((bsz, n_heads, DK, DV), f32),
            jax.ShapeDtypeStruct((bsz, SUBLANES, conv_w), f32),
        ],
        scratch_shapes=[
            pltpu.VMEM((rows + SUBLANES, conv_w), f32),
            pltpu.VMEM((rows, conv_w), f32),
            pltpu.VMEM((rows, hv), f32),
            pltpu.VMEM((n_heads, DK, DV), f32),
        ],
        compiler_params=pltpu.CompilerParams(
            dimension_semantics=("parallel", "arbitrary"), vmem_limit_bytes=VMEM_LIMIT),
        name="gdn_prompt",
    )(x, g, *ws, cw, dtb, aneg, nw)


def _conv_step(u, cs_ref, cs_out_ref, cw_ref, cb_ref):
    cw = cw_ref[...]
    conv = u * cw[3:4]
    for j in range(CONV_K - 1):
        conv = conv + cs_ref[j] * cw[j:j + 1]
    if cb_ref is not None:
        conv = conv + cb_ref[...]
    for j in range(CONV_K - 2):
        cs_out_ref[j] = cs_ref[j + 1]
    cs_out_ref[CONV_K - 2] = u
    return _silu(conv)


def _ssd_step_kernel(xbc_ref, z_ref, dt_ref, cs_ref, st_ref, cw_ref, cb_ref, dtb_ref, aneg_ref, dexp_ref,
                     nw_ref, e_ref, y_ref, cs_out_ref, st_out_ref, yacc_ref):
    bb = xbc_ref.shape[0]
    d_inner = y_ref.shape[1]
    gw = d_inner // G_M
    act = _conv_step(xbc_ref[...], cs_ref, cs_out_ref, cw_ref, cb_ref)
    z = z_ref[...]
    dt = _softplus(dt_ref[...] + dtb_ref[...])
    dec = jnp.exp(dt * aneg_ref[...])
    se = _mm(_pack3_lanes(jnp.concatenate([dt, dec], axis=0), d_inner // P_M), e_ref[...])
    xs = act[:, :d_inner]
    xdt = xs * se[0:bb]
    dec_e = se[bb:2 * bb]
    d3 = jnp.concatenate(_split3(dec_e), axis=0)
    r3, c3 = _iota2((3 * bb, bb * N_M))
    seq3 = c3 // N_M
    onehot3 = jnp.where((r3 == seq3) | (r3 == seq3 + bb) | (r3 == seq3 + 2 * bb), 1.0, 0.0).astype(bf16)
    rb, cb_i = _iota2((bb, bb * N_M))
    own = rb == cb_i // N_M
    rowid = rb[:, 0:1]

    for g in range(G_M):
        gs = slice(g * gw, (g + 1) * gw)
        bm = act[:, d_inner + g * N_M:d_inner + (g + 1) * N_M]
        cm = act[:, d_inner + (G_M + g) * N_M:d_inner + (G_M + g + 1) * N_M].astype(bf16)
        bdiag = jnp.where(own, jnp.concatenate([bm] * bb, axis=1), 0.0).astype(bf16)
        dcol = _tn(d3[:, gs], onehot3)
        outer = _tn(xdt[:, gs].astype(bf16), bdiag)
        yg = None
        for b in range(bb):
            bs = slice(b * N_M, (b + 1) * N_M)
            s_new = st_ref[b, gs, :] * dcol[:, bs] + outer[:, bs]
            st_out_ref[b, gs, :] = s_new
            yb = _nt(cm, s_new.astype(bf16))
            yg = yb if yg is None else jnp.where(rowid == b, yb, yg)
        yacc_ref[:, gs] = yg

    y = yacc_ref[...] + xs * dexp_ref[...]
    y = y * _silu(z)
    for g in range(G_M):
        gs = slice(g * gw, (g + 1) * gw)
        y_ref[:, gs] = _rms(y[:, gs], nw_ref[:, gs]).astype(y_ref.dtype)


def _ssd_sample(projs, conv_state, state, cw, cb, dtb, aneg, dexp, nw, e_mat):
    bsz = projs[0].shape[0]
    d_inner = dexp.shape[1]
    conv_w = cw.shape[1]
    bb = SAMPLE_BB
    const = lambda a: pl.BlockSpec(a.shape, lambda i: (0, 0))
    return pl.pallas_call(
        _ssd_step_kernel,
        grid=(bsz // bb,),
        in_specs=[pl.BlockSpec((bb, p.shape[1]), lambda i: (i, 0)) for p in projs] + [
            pl.BlockSpec((CONV_K - 1, bb, conv_w), lambda i: (0, i, 0)),
            pl.BlockSpec((bb, d_inner, N_M), lambda i: (i, 0, 0)),
        ] + [const(a) for a in (cw, cb, dtb, aneg, dexp, nw, e_mat)],
        out_specs=[
            pl.BlockSpec((bb, d_inner), lambda i: (i, 0)),
            pl.BlockSpec((CONV_K - 1, bb, conv_w), lambda i: (0, i, 0)),
            pl.BlockSpec((bb, d_inner, N_M), lambda i: (i, 0, 0)),
        ],
        out_shape=[
            jax.ShapeDtypeStruct((bsz, d_inner), bf16),
            jax.ShapeDtypeStruct((CONV_K - 1, bsz, conv_w), f32),
            jax.ShapeDtypeStruct((bsz, d_inner, N_M), f32),
        ],
        scratch_shapes=[pltpu.VMEM((bb, d_inner), f32)],
        compiler_params=pltpu.CompilerParams(
            dimension_semantics=("parallel",), vmem_limit_bytes=VMEM_LIMIT),
        name="ssd_sample",
    )(*projs, conv_state, state, cw, cb, dtb, aneg, dexp, nw, e_mat)


def _gdn_step_kernel(qkv_ref, gate_ref, ba_ref, cs_ref, st_ref, cw_ref, dtb_ref, aneg_ref, nw_ref,
                     o_ref, cs_out_ref, st_out_ref):
    bb = qkv_ref.shape[0]
    n_heads = st_ref.shape[1]
    hk = n_heads * DK
    act = _conv_step(qkv_ref[...], cs_ref, cs_out_ref, cw_ref, None)
    beta = jax.nn.sigmoid(ba_ref[:, :LANES])
    eg = jnp.exp(aneg_ref[...] * _softplus(ba_ref[:, LANES:] + dtb_ref[...]))
    rb, cb_i = _iota2((bb, bb * DV))
    own = rb == cb_i // DV
    rowid = rb[:, 0:1]

    for h in range(n_heads):
        q = act[:, h * DK:(h + 1) * DK]
        k = act[:, hk + h * DK:hk + (h + 1) * DK]
        v = act[:, 2 * hk + h * DV:2 * hk + (h + 1) * DV]
        qn = q * lax.rsqrt(jnp.sum(q * q, axis=-1, keepdims=True) + EPS) * (DK ** -0.5)
        kn = k * lax.rsqrt(jnp.sum(k * k, axis=-1, keepdims=True) + EPS)
        bcol = beta[:, h:h + 1]
        ecol = eg[:, h:h + 1]
        kb = kn * bcol
        u = v * bcol
        qk = jnp.sum(qn.astype(bf16).astype(f32) * kn.astype(bf16).astype(f32), axis=-1, keepdims=True)
        wq = jnp.concatenate([(kb * ecol).astype(bf16), (qn * ecol).astype(bf16)], axis=0)
        vn = o = None
        for b in range(bb):
            ws = _mm(wq, st_ref[b, h].astype(bf16))
            vn_b = u - ws[:bb]
            o_b = ws[bb:] + qk * vn_b
            vn = vn_b if vn is None else jnp.where(rowid == b, vn_b, vn)
            o = o_b if o is None else jnp.where(rowid == b, o_b, o)
        vdiag = jnp.where(own, jnp.concatenate([vn] * bb, axis=1), 0.0).astype(bf16)
        outer = _tn(kn.astype(bf16), vdiag)
        for b in range(bb):
            st_out_ref[b, h] = st_ref[b, h] * ecol[b:b + 1, :] + outer[:, b * DV:(b + 1) * DV]
        gt = gate_ref[:, h * DV:(h + 1) * DV]
        o_ref[:, h * DV:(h + 1) * DV] = (_rms(o, nw_ref[...]) * _silu(gt)).astype(o_ref.dtype)


def _gdn_sample(projs, conv_state, state, cw, dtb, aneg, nw):
    bsz = projs[0].shape[0]
    n_heads = state.shape[1]
    conv_w = cw.shape[1]
    hv = n_heads * DV
    bb = SAMPLE_BB_GDN
    const = lambda a: pl.BlockSpec(a.shape, lambda i: (0, 0))
    return pl.pallas_call(
        _gdn_step_kernel,
        grid=(bsz // bb,),
        in_specs=[pl.BlockSpec((bb, p.shape[1]), lambda i: (i, 0)) for p in projs] + [
            pl.BlockSpec((CONV_K - 1, bb, conv_w), lambda i: (0, i, 0)),
            pl.BlockSpec((bb, n_heads, DK, DV), lambda i: (i, 0, 0, 0)),
        ] + [const(a) for a in (cw, dtb, aneg, nw)],
        out_specs=[
            pl.BlockSpec((bb, hv), lambda i: (i, 0)),
            pl.BlockSpec((CONV_K - 1, bb, conv_w), lambda i: (0, i, 0)),
            pl.BlockSpec((bb, n_heads, DK, DV), lambda i: (i, 0, 0, 0)),
        ],
        out_shape=[
            jax.ShapeDtypeStruct((bsz, hv), bf16),
            jax.ShapeDtypeStruct((CONV_K - 1, bsz, conv_w), f32),
            jax.ShapeDtypeStruct((bsz, n_heads, DK, DV), f32),
        ],
        compiler_params=pltpu.CompilerParams(
            dimension_semantics=("parallel",), vmem_limit_bytes=VMEM_LIMIT),
        name="gdn_sample",
    )(*projs, conv_state, state, cw, dtb, aneg, nw)


def _tail_kernel(x_ref, y_ref, o_ref, p_ref, nmix_ref, wmg_ref, wbs_ref, wbg_ref, wout_ref, nf_ref, wfi_ref,
                 wfo_ref, npl_ref, wpg_ref, wpp_ref, nfin_ref, out_ref, *, ff_chunks, final_norm):
    d = x_ref.shape[1]
    d_ff = wfo_ref.shape[0]
    x = x_ref[...]
    mg = _nt(_rms(x, nmix_ref[...]).astype(bf16), wmg_ref[...])
    mix = (jax.nn.sigmoid(mg[:, :d]) * _mm(y_ref[...], wbs_ref[...])
           + jax.nn.sigmoid(mg[:, d:]) * _mm(o_ref[...], wbg_ref[...]))
    x = x + _mm(mix.astype(bf16), wout_ref[...])
    h = _rms(x, nf_ref[...]).astype(bf16)
    fc = d_ff // ff_chunks
    for c in range(ff_chunks):
        gt = _mm(h, wfi_ref[:, c * fc:(c + 1) * fc])
        up = _mm(h, wfi_ref[:, d_ff + c * fc:d_ff + (c + 1) * fc])
        x = x + _mm((_silu(gt) * up).astype(bf16), wfo_ref[c * fc:(c + 1) * fc, :])
    pe = _mm(p_ref[...].astype(bf16), wpp_ref[...])
    x = x + pe * jax.nn.sigmoid(_mm(_rms(x, npl_ref[...]).astype(bf16), wpg_ref[...]))
    if final_norm:
        x = _rms(x, nfin_ref[...])
    out_ref[...] = x


def _tail(x, y, o, p, wts, tm, final_norm):
    t, d = x.shape
    tok = lambda w: pl.BlockSpec((tm, w), lambda i: (i, 0))
    res = lambda a: pl.BlockSpec(a.shape, lambda i: (0, 0), pipeline_mode=pl.Buffered(1))
    names = ("nmix", "wmg", "wbs", "wbg", "wout", "nf", "wfi", "wfo", "npl", "wpg", "wpp", "nfin")
    return pl.pallas_call(
        functools.partial(_tail_kernel, ff_chunks=2, final_norm=final_norm),
        grid=(t // tm,),
        in_specs=[tok(d), tok(y.shape[1]), tok(o.shape[1]), tok(p.shape[1])] + [res(wts[n]) for n in names],
        out_specs=tok(d),
        out_shape=jax.ShapeDtypeStruct((t, d), f32),
        compiler_params=pltpu.CompilerParams(
            dimension_semantics=("parallel",), vmem_limit_bytes=VMEM_LIMIT),
        name="tail",
    )(x, y, o, p, *[wts[n] for n in names])


def _pad_lanes(v, width=LANES):
    return jnp.pad(v, ((0, 0), (0, width - v.shape[1])))


def _pad_rows(v, height=LANES):
    return jnp.pad(v, ((0, height - v.shape[0]), (0, 0)))


def _layer_weights(i, d_model, norm_mix, w_in, ssm_conv_w, ssm_conv_b, ssm_dt_bias, ssm_a_log, ssm_d, ssm_norm,
                   gdn_conv_w, gdn_dt_bias, gdn_a_log, gdn_norm, w_branch_ssm, w_branch_gdn, w_out,
                   norm_ffn, w_ffn_in, w_ffn_out, norm_pl, w_pl_gate, w_pl_proj, norm_final):
    h_m = ssm_dt_bias.shape[1]
    h_g = gdn_dt_bias.shape[1]
    d_inner = h_m * P_M
    conv_m = ssm_conv_w.shape[2]
    conv_g = gdn_conv_w.shape[2]
    sizes = (d_inner, conv_m, h_m, conv_g, h_g * DV, h_g, h_g, 2 * d_model)
    starts = [0]
    for s in sizes:
        starts.append(starts[-1] + s)
    w_t = jnp.swapaxes(w_in[i], 0, 1)
    seg = lambda j: w_t[starts[j]:starts[j + 1]].astype(bf16)
    z, xbc, dtw, qkv, gate, bw, aw, mg = (seg(j) for j in range(8))
    w_ssd = (xbc, z, _pad_rows(dtw))
    w_gdn = (qkv, gate, jnp.concatenate([_pad_rows(bw), _pad_rows(aw)], axis=0))
    head_of = jnp.arange(d_inner, dtype=jnp.int32) // P_M
    assert 3 * h_m <= LANES
    rows_e = jnp.arange(LANES, dtype=jnp.int32)[:, None]
    e_mat = ((rows_e % h_m == head_of[None, :]) & (rows_e < 3 * h_m)).astype(bf16)
    row = lambda v: v.reshape(1, -1).astype(f32)
    return dict(
        norm_mix=row(norm_mix[i]), w_ssd=w_ssd, w_gdn=w_gdn,
        ssm=dict(cw=ssm_conv_w[i], cb=row(ssm_conv_b[i]), dtb=_pad_lanes(row(ssm_dt_bias[i])),
                 aneg=_pad_lanes(-jnp.exp(row(ssm_a_log[i]))), dexp=row(jnp.repeat(ssm_d[i], P_M)),
                 nw=row(ssm_norm[i]), e_mat=e_mat),
        gdn=dict(cw=gdn_conv_w[i], dtb=_pad_lanes(row(gdn_dt_bias[i])),
                 aneg=_pad_lanes(-jnp.exp(row(gdn_a_log[i]))), nw=row(gdn_norm[i])),
        tail=dict(nmix=row(norm_mix[i]), wmg=mg,
                  wbs=w_branch_ssm[i].astype(bf16), wbg=w_branch_gdn[i].astype(bf16), wout=w_out[i].astype(bf16),
                  nf=row(norm_ffn[i]), wfi=w_ffn_in[i].astype(bf16), wfo=w_ffn_out[i].astype(bf16),
                  npl=row(norm_pl[i]), wpg=w_pl_gate[i].astype(bf16), wpp=w_pl_proj[i].astype(bf16),
                  nfin=row(norm_final)),
    )


def _pick_tile(n, candidates):
    for c in candidates:
        if n % c == 0:
            return c
    raise ValueError(f"no tile for {n}")


def kernel(x_prompt, x_sample, p_prompt, p_sample, state_ssm, state_ssm_conv, state_gdn, state_gdn_conv,
           norm_mix, w_in, ssm_conv_w, ssm_conv_b, ssm_dt_bias, ssm_a_log, ssm_d, ssm_norm,
           gdn_conv_w, gdn_dt_bias, gdn_a_log, gdn_norm, w_branch_ssm, w_branch_gdn, w_out,
           norm_ffn, w_ffn_in, w_ffn_out, norm_pl, w_pl_gate, w_pl_proj, norm_final):
    depth = p_prompt.shape[0]
    bp, seq, d_model = x_prompt.shape
    bs, dec_seq, _ = x_sample.shape
    assert dec_seq == 1 and seq % STEP_TOKENS == 0 and bs % SAMPLE_BB == 0 and bs % SAMPLE_BB_GDN == 0
    h_m, h_g = ssm_dt_bias.shape[1], gdn_dt_bias.shape[1]
    d_inner = h_m * P_M
    tail_rows = slice(SUBLANES - (CONV_K - 1), SUBLANES)

    xp = x_prompt.reshape(bp * seq, d_model)
    xs = x_sample.reshape(bs, d_model)
    new_p = ([], [], [], [])
    new_s = ([], [], [], [])
    for i in range(depth):
        lw = _layer_weights(i, d_model, norm_mix, w_in, ssm_conv_w, ssm_conv_b, ssm_dt_bias, ssm_a_log, ssm_d,
                            ssm_norm, gdn_conv_w, gdn_dt_bias, gdn_a_log, gdn_norm, w_branch_ssm, w_branch_gdn,
                            w_out, norm_ffn, w_ffn_in, w_ffn_out, norm_pl, w_pl_gate, w_pl_proj, norm_final)
        last = i == depth - 1

        tp = bp * seq
        y, st_ssm, cs_ssm = _ssd_prompt(xp, lw["norm_mix"], lw["w_ssd"], bp, seq, **lw["ssm"])
        o, st_gdn, cs_gdn = _gdn_prompt(xp, lw["norm_mix"], lw["w_gdn"], bp, seq, n_heads=h_g, **lw["gdn"])
        xp = _tail(xp, y, o, p_prompt[i].reshape(tp, -1), lw["tail"], _pick_tile(tp, (256, 128)), last)
        new_p[0].append(st_ssm.reshape(bp, h_m, P_M, N_M))
        new_p[1].append(cs_ssm[:, tail_rows, :])
        new_p[2].append(st_gdn)
        new_p[3].append(cs_gdn[:, tail_rows, :])

        ys, cs_s, ss_s = _ssd_sample(_inproj(xs, lw["norm_mix"], lw["w_ssd"]),
                                     jnp.swapaxes(state_ssm_conv[i], 0, 1),
                                     state_ssm[i].reshape(bs, d_inner, N_M), **lw["ssm"])
        os_, cs_g, ss_g = _gdn_sample(_inproj(xs, lw["norm_mix"], lw["w_gdn"]),
                                      jnp.swapaxes(state_gdn_conv[i], 0, 1), state_gdn[i], **lw["gdn"])
        xs = _tail(xs, ys, os_, p_sample[i].reshape(bs, -1), lw["tail"],
                   _pick_tile(bs, (128, 64, 32, 16, 8)), last)
        new_s[0].append(ss_s.reshape(bs, h_m, P_M, N_M))
        new_s[1].append(jnp.swapaxes(cs_s, 0, 1))
        new_s[2].append(ss_g)
        new_s[3].append(jnp.swapaxes(cs_g, 0, 1))

    stack = lambda lst: jnp.stack(lst)
    return (xp.reshape(bp, seq, d_model), xs.reshape(bs, 1, d_model),
            stack(new_p[0]), stack(new_p[1]), stack(new_p[2]), stack(new_p[3]),
            stack(new_s[0]), stack(new_s[1]), stack(new_s[2]), stack(new_s[3]))
```

```python
import functools

import jax
import jax.numpy as jnp
from jax import lax
from jax.experimental import pallas as pl
from jax.experimental.pallas import tpu as pltpu

f32 = jnp.float32
bf16 = jnp.bfloat16

EPS = 1e-6
CONV_K = 4
LANES = 128
SUBLANES = 8
MXU_DIM = 256
VMEM_LIMIT = 56 * 1024 * 1024

P_M = 64
N_M = 128
G_M = 4
DK = 128
DV = 128
SSD_BLOCK = 128
GDN_CHUNK = 64
GDN_BLOCK = 128
GDN_BASE = 16
STEP_TOKENS = 512
CONV_COLS = 512
SAMPLE_BB = 8
SAMPLE_BB_GDN = 16


def _nt(a, b):
    return lax.dot_general(a, b, (((1,), (1,)), ((), ())), preferred_element_type=f32)


def _tn(a, b):
    return lax.dot_general(a, b, (((0,), (0,)), ((), ())), preferred_element_type=f32)


def _mm(a, b):
    return jnp.dot(a, b, preferred_element_type=f32)


def _split3(x):
    hi = x.astype(bf16)
    r1 = x - hi.astype(f32)
    mid = r1.astype(bf16)
    lo = (r1 - mid.astype(f32)).astype(bf16)
    return hi, mid, lo


def _mm_sel_rhs(x, sel3):
    return _mm(jnp.concatenate(_split3(x), axis=1), sel3)


def _mm_sel_lhs(sel, x):
    return _mm(jnp.concatenate([sel] * 3, axis=1), jnp.concatenate(_split3(x), axis=0))


def _pack3_lanes(x, width):
    lane = lax.broadcasted_iota(jnp.int32, x.shape, 1)
    hi, mid, lo = (t.astype(f32) for t in _split3(jnp.where(lane < width, x, 0.0)))
    return (hi + pltpu.roll(mid, width, 1) + pltpu.roll(lo, 2 * width, 1)).astype(bf16)


def _silu(x):
    h = 0.5 * x
    return h + h * jnp.tanh(h)


def _softplus(x):
    return jnp.maximum(x, 0.0) + jnp.log1p(jnp.exp(-jnp.abs(x)))


def _rms(x, g):
    return x * lax.rsqrt(jnp.mean(x * x, axis=-1, keepdims=True) + EPS) * g


def _iota2(shape):
    return lax.broadcasted_iota(jnp.int32, shape, 0), lax.broadcasted_iota(jnp.int32, shape, 1)


def _inproj_kernel(x_ref, g_ref, *refs):
    n = len(refs) // 2
    h = _rms(x_ref[...], g_ref[...]).astype(bf16)
    for w_ref, o_ref in zip(refs[:n], refs[n:]):
        o_ref[...] = _nt(h, w_ref[...])


def _inproj(x, g, ws):
    t, d = x.shape
    full = lambda shape: pl.BlockSpec(shape, lambda i: (0, 0))
    return pl.pallas_call(
        _inproj_kernel,
        grid=(1,),
        in_specs=[full((t, d)), full((1, d))] + [full(w.shape) for w in ws],
        out_specs=[full((t, w.shape[0])) for w in ws],
        out_shape=[jax.ShapeDtypeStruct((t, w.shape[0]), f32) for w in ws],
        compiler_params=pltpu.CompilerParams(dimension_semantics=("arbitrary",), vmem_limit_bytes=VMEM_LIMIT),
        name="inproj",
    )(x, g, *ws)


def _project_conv(h, w_ref, ext_ref, act_ref, cw_ref, cb_ref, rows, width):
    slabs = [slice(c0, c0 + CONV_COLS) for c0 in range(0, width, CONV_COLS)]
    u_next = _nt(h, w_ref[slabs[0], :])
    for k, sl in enumerate(slabs):
        u = u_next
        if k + 1 < len(slabs):
            u_next = _nt(h, w_ref[slabs[k + 1], :])
        ext_ref[SUBLANES:SUBLANES + rows, sl] = u
        cw = cw_ref[:, sl]
        conv = u * cw[3:4]
        full = ext_ref[:, sl]
        for j in range(CONV_K - 1):
            shifted = pltpu.roll(full, CONV_K - 1 - j, 0)
            conv = conv + shifted[SUBLANES:SUBLANES + rows] * cw[j:j + 1]
        if cb_ref is not None:
            conv = conv + cb_ref[:, sl]
        act_ref[:, sl] = _silu(conv)
        ext_ref[0:SUBLANES, sl] = ext_ref[rows:rows + SUBLANES, sl]


def _ssd_kernel(x_ref, g_ref, wx_ref, wz_ref, wdt_ref, cw_ref, cb_ref, dtb_ref, aneg_ref, dexp_ref, nw_ref, e_ref,
                y_ref, st_out_ref, cs_out_ref, ext_ref, act_ref, z_ref, st_ref, yd_ref):
    i = pl.program_id(1)
    rows = x_ref.shape[0]
    c = SSD_BLOCK
    conv_w = act_ref.shape[1]
    d_inner = y_ref.shape[1]
    gw = d_inner // G_M
    hg = gw // P_M
    n_heads = d_inner // P_M

    @pl.when(i == 0)
    def _():
        ext_ref[0:SUBLANES, :] = jnp.zeros((SUBLANES, conv_w), f32)
        st_ref[...] = jnp.zeros(st_ref.shape, f32)

    h = _rms(x_ref[...], g_ref[...]).astype(bf16)
    _project_conv(h, wx_ref, ext_ref, act_ref, cw_ref, cb_ref, rows, conv_w)
    dt_raw = _nt(h, wdt_ref[...])
    for g in range(G_M):
        z_ref[:, g * gw:(g + 1) * gw] = _nt(h, wz_ref[g * gw:(g + 1) * gw, :])

    r, cc = _iota2((c, c))
    lower = r >= cc
    lower_b = jnp.where(lower, 1.0, 0.0).astype(bf16)
    upper3 = jnp.concatenate([jnp.where(r <= cc, 1.0, 0.0).astype(bf16)] * 3, axis=0)

    for sub in range(rows // c):
        rs = slice(sub * c, (sub + 1) * c)
        dt = _softplus(dt_raw[rs] + dtb_ref[...])
        da = dt * aneg_ref[...]
        acum = _mm_sel_lhs(lower_b, da)
        acum_t = _mm_sel_rhs(da.T, upper3)
        alast = acum[c - 1:c, :]
        dout = jnp.exp(alast - acum)
        scal = _pack3_lanes(jnp.concatenate(
            [dt, dt * dout, jnp.exp(acum), jnp.broadcast_to(jnp.exp(alast), (SUBLANES, LANES))], axis=0), n_heads)
        for g in range(G_M):
            gs = slice(g * gw, (g + 1) * gw)
            se = _mm(scal, e_ref[:, gs])
            dt_e, dd_e, ea_e, cd_e = se[0:c], se[c:2 * c], se[2 * c:3 * c], se[3 * c:3 * c + 1]
            xs = act_ref[rs, gs]
            bm = act_ref[rs, d_inner + g * N_M:d_inner + (g + 1) * N_M].astype(bf16)
            cm = act_ref[rs, d_inner + (G_M + g) * N_M:d_inner + (G_M + g + 1) * N_M].astype(bf16)
            xdt = (xs * dt_e).astype(bf16)
            cb = _nt(cm, bm)
            st = st_ref[:, gs]
            y_off = _mm(cm, st.astype(bf16)) * ea_e
            st_ref[:, gs] = st * cd_e + _tn(bm, (xs * dd_e).astype(bf16))
            for hh in range(hg):
                hd = g * hg + hh
                lmat = jnp.exp(jnp.where(lower, acum[:, hd:hd + 1] - acum_t[hd:hd + 1, :], -jnp.inf))
                yd_ref[:, hh * P_M:(hh + 1) * P_M] = _mm((cb * lmat).astype(bf16), xdt[:, hh * P_M:(hh + 1) * P_M])
            y = yd_ref[...] + y_off + xs * dexp_ref[:, gs]
            y = y * _silu(z_ref[rs, gs])
            y_ref[rs, gs] = _rms(y, nw_ref[:, gs]).astype(y_ref.dtype)

    @pl.when(i == pl.num_programs(1) - 1)
    def _():
        cs_out_ref[0] = ext_ref[0:SUBLANES, :]
        for j in range(d_inner // LANES):
            st_out_ref[0, j * LANES:(j + 1) * LANES, :] = st_ref[:, j * LANES:(j + 1) * LANES].T


def _ssd_prompt(x, g, ws, bsz, seq, cw, cb, dtb, aneg, dexp, nw, e_mat):
    d_model = x.shape[1]
    d_inner = dexp.shape[1]
    conv_w = cw.shape[1]
    rows = STEP_TOKENS
    nb = seq // rows
    tok = lambda b, i: (b * nb + i, 0)
    res = lambda a: pl.BlockSpec(a.shape, lambda b, i: (0, 0), pipeline_mode=pl.Buffered(1))
    return pl.pallas_call(
        _ssd_kernel,
        grid=(bsz, nb),
        in_specs=[pl.BlockSpec((rows, d_model), tok)] + [res(a) for a in (g, *ws, cw, cb, dtb, aneg, dexp, nw, e_mat)],
        out_specs=[
            pl.BlockSpec((rows, d_inner), tok),
            pl.BlockSpec((1, d_inner, N_M), lambda b, i: (b, 0, 0)),
            pl.BlockSpec((1, SUBLANES, conv_w), lambda b, i: (b, 0, 0)),
        ],
        out_shape=[
            jax.ShapeDtypeStruct((bsz * seq, d_inner), bf16),
            jax.ShapeDtypeStruct((bsz, d_inner, N_M), f32),
            jax.ShapeDtypeStruct((bsz, SUBLANES, conv_w), f32),
        ],
        scratch_shapes=[
            pltpu.VMEM((rows + SUBLANES, conv_w), f32),
            pltpu.VMEM((rows, conv_w), f32),
            pltpu.VMEM((rows, d_inner), f32),
            pltpu.VMEM((N_M, d_inner), f32),
            pltpu.VMEM((SSD_BLOCK, d_inner // G_M), f32),
        ],
        compiler_params=pltpu.CompilerParams(
            dimension_semantics=("parallel", "arbitrary"), vmem_limit_bytes=VMEM_LIMIT),
        name="ssd_prompt",
    )(x, g, *ws, cw, cb, dtb, aneg, dexp, nw, e_mat)


def _gdn_kernel(x_ref, g_ref, wx_ref, wgate_ref, wba_ref, cw_ref, dtb_ref, aneg_ref, nw_ref,
                o_ref, st_out_ref, cs_out_ref, ext_ref, act_ref, gate_ref, st_ref):
    i = pl.program_id(1)
    rows = x_ref.shape[0]
    blk = GDN_BLOCK
    ck = GDN_CHUNK
    conv_w = act_ref.shape[1]
    n_heads = st_ref.shape[0]
    hk = n_heads * DK
    per_blk = blk // ck
    heads = range(n_heads)

    @pl.when(i == 0)
    def _():
        ext_ref[0:SUBLANES, :] = jnp.zeros((SUBLANES, conv_w), f32)
        st_ref[...] = jnp.zeros(st_ref.shape, f32)

    h_in = _rms(x_ref[...], g_ref[...]).astype(bf16)
    _project_conv(h_in, wx_ref, ext_ref, act_ref, cw_ref, None, rows, conv_w)
    ba = _nt(h_in, wba_ref[...])
    gate_ref[...] = _nt(h_in, wgate_ref[...])

    r, cc = _iota2((blk, blk))
    same = (r // ck) == (cc // ck)
    lower = same & (r >= cc)
    strict = same & (r > cc)
    lower_b = jnp.where(lower, 1.0, 0.0).astype(bf16)
    upper3 = jnp.concatenate([jnp.where(same & (r <= cc), 1.0, 0.0).astype(bf16)] * 3, axis=0)
    rcol = r[:, 0:1]
    eye = jnp.where(r == cc, 1.0, 0.0)
    blk_masks = []
    size = GDN_BASE
    while size <= ck:
        blk_masks.append((r // size) == (cc // size))
        size *= 2

    n_sub = rows // blk
    prep = [None] * n_sub
    xs = [None] * n_sub
    qks, qgs, kgs, gls = ([None] * (n_sub * n_heads) for _ in range(4))

    def prepare(sub):
        rs = slice(sub * blk, (sub + 1) * blk)
        beta = jax.nn.sigmoid(ba[rs, :LANES])
        glog = aneg_ref[...] * _softplus(ba[rs, LANES:] + dtb_ref[...])
        gcum = _mm_sel_lhs(lower_b, glog)
        gcum_t = _mm_sel_rhs(glog.T, upper3)
        nmats, rhs = [], []
        for h in heads:
            q = act_ref[rs, h * DK:(h + 1) * DK]
            k = act_ref[rs, hk + h * DK:hk + (h + 1) * DK]
            v = act_ref[rs, 2 * hk + h * DV:2 * hk + (h + 1) * DV]
            qn = q * lax.rsqrt(jnp.sum(q * q, axis=-1, keepdims=True) + EPS) * (DK ** -0.5)
            kn = k * lax.rsqrt(jnp.sum(k * k, axis=-1, keepdims=True) + EPS)
            bcol = beta[:, h:h + 1]
            gcol = gcum[:, h:h + 1]
            eg = jnp.exp(gcol)
            dm = jnp.exp(jnp.where(lower, gcol - gcum_t[h:h + 1, :], -jnp.inf))
            knb = kn.astype(bf16)
            kb = kn * bcol
            nmats.append(jnp.where(strict, -(_nt(kb.astype(bf16), knb) * dm), 0.0))
            rhs.append(jnp.concatenate([v * bcol, kb * eg], axis=-1))
            glast = [gcol[(j + 1) * ck - 1:(j + 1) * ck] for j in range(per_blk)]
            glast_row = glast[per_blk - 1]
            for j in range(per_blk - 2, -1, -1):
                glast_row = jnp.where(rcol < (j + 1) * ck, glast[j], glast_row)
            idx = sub * n_heads + h
            qks[idx] = jnp.where(lower, _nt(qn.astype(bf16), knb) * dm, 0.0).astype(bf16)
            qgs[idx] = (qn * eg).astype(bf16)
            kgs[idx] = (kn * jnp.exp(glast_row - gcol)).astype(bf16)
            gls[idx] = [jnp.exp(gl) for gl in glast]
            yield
        prep[sub] = (nmats, rhs)

    def solve(sub):
        nmats, rhs = prep[sub]
        qs = [jnp.where(blk_masks[0], n, 0.0) for n in nmats]
        minv = [eye + q for q in qs]
        span = 2
        while span < GDN_BASE:
            qbs = [q.astype(bf16) for q in qs]
            qs = [_mm(qb, qb) for qb in qbs]
            yield
            minv = [m + _mm(m.astype(bf16), q.astype(bf16)) for m, q in zip(minv, qs)]
            yield
            span *= 2
        for lvl in range(1, len(blk_masks)):
            off = blk_masks[lvl] & jnp.logical_not(blk_masks[lvl - 1])
            mbs = [m.astype(bf16) for m in minv]
            ems = [_mm(jnp.where(off, n, 0.0).astype(bf16), mb) for n, mb in zip(nmats, mbs)]
            yield
            minv = [m + _mm(mb, em.astype(bf16)) for m, mb, em in zip(minv, mbs, ems)]
            yield
        xs[sub] = [_mm(m.astype(bf16), x.astype(bf16)) for m, x in zip(minv, rhs)]

    state = {"s": [st_ref[h] for h in heads]}
    zeros = jnp.zeros((ck, DV), bf16)

    def recur(sub):
        for j in range(per_blk):
            rl = slice(j * ck, (j + 1) * ck)
            ro = slice(sub * blk + j * ck, sub * blk + (j + 1) * ck)
            at = lambda lst, h: lst[sub * n_heads + h]
            ss = state["s"]
            sbs = [s.astype(bf16) for s in ss]
            vns = [xs[sub][h][rl, :DV] - _mm(xs[sub][h][rl, DV:].astype(bf16), sbs[h]) for h in heads]
            yield
            vfull = [jnp.concatenate([zeros] * j + [vns[h].astype(bf16)] + [zeros] * (per_blk - 1 - j), axis=0)
                     for h in heads]
            outs = [_mm(at(qgs, h)[rl], sbs[h]) + _mm(at(qks, h)[rl, :], vfull[h]) for h in heads]
            yield
            state["s"] = [ss[h] * at(gls, h)[j] + _tn(at(kgs, h)[rl], vns[h].astype(bf16)) for h in heads]
            yield
            for h in heads:
                gt = gate_ref[ro, h * DV:(h + 1) * DV]
                o_ref[ro, h * DV:(h + 1) * DV] = (_rms(outs[h], nw_ref[...]) * _silu(gt)).astype(o_ref.dtype)
            yield

    def emit(*gens):
        live = list(gens)
        while live:
            for gen in list(live):
                if next(gen, StopIteration) is StopIteration:
                    live.remove(gen)

    emit(prepare(0))
    for sub in range(n_sub):
        later = [prepare(sub + 1)] if sub + 1 < n_sub else []
        earlier = [recur(sub - 1)] if sub > 0 else []
        emit(solve(sub), *later, *earlier)
    emit(recur(n_sub - 1))
    for h in heads:
        st_ref[h] = state["s"][h]

    @pl.when(i == pl.num_programs(1) - 1)
    def _():
        cs_out_ref[0] = ext_ref[0:SUBLANES, :]
        st_out_ref[0] = st_ref[...]


def _gdn_prompt(x, g, ws, bsz, seq, cw, dtb, aneg, nw, n_heads):
    d_model = x.shape[1]
    conv_w = cw.shape[1]
    hv = n_heads * DV
    rows = STEP_TOKENS
    nb = seq // rows
    tok = lambda b, i: (b * nb + i, 0)
    res = lambda a: pl.BlockSpec(a.shape, lambda b, i: (0, 0), pipeline_mode=pl.Buffered(1))
    return pl.pallas_call(
        _gdn_kernel,
        grid=(bsz, nb),
        in_specs=[pl.BlockSpec((rows, d_model), tok)] + [res(a) for a in (g, *ws, cw, dtb, aneg, nw)],
        out_specs=[
            pl.BlockSpec((rows, hv), tok),
            pl.BlockSpec((1, n_heads, DK, DV), lambda b, i: (b, 0, 0, 0)),
            pl.BlockSpec((1, SUBLANES, conv_w), lambda b, i: (b, 0, 0)),
        ],
        out_shape=[
            jax.ShapeDtypeStruct((bsz * seq, hv), bf16),
            jax.ShapeDtypeStruct((bsz, n_heads, DK, DV), f32),
            jax.ShapeDtypeStruct((bsz, SUBLANES, conv_w), f32),
        ],
        scratch_shapes=[
            pltpu.VMEM((rows + SUBLANES, conv_w), f32),
            pltpu.VMEM((rows, conv_w), f32),
            pltpu.VMEM((rows, hv), f32),
            pltpu.VMEM((n_heads, DK, DV), f32),
        ],
        compiler_params=pltpu.CompilerParams(
            dimension_semantics=("parallel", "arbitrary"), vmem_limit_bytes=VMEM_LIMIT),
        name="gdn_prompt",
    )(x, g, *ws, cw, dtb, aneg, nw)


def _conv_step(u, cs_ref, cs_out_ref, cw_ref, cb_ref):
    cw = cw_ref[...]
    conv = u * cw[3:4]
    for j in range(CONV_K - 1):
        conv = conv + cs_ref[j] * cw[j:j + 1]
    if cb_ref is not None:
        conv = conv + cb_ref[...]
    for j in range(CONV_K - 2):
        cs_out_ref[j] = cs_ref[j + 1]
    cs_out_ref[CONV_K - 2] = u
    return _silu(conv)


def _ssd_step_kernel(xbc_ref, z_ref, dt_ref, cs_ref, st_ref, cw_ref, cb_ref, dtb_ref, aneg_ref, dexp_ref,
                     nw_ref, e_ref, y_ref, cs_out_ref, st_out_ref, yacc_ref):
    bb = xbc_ref.shape[0]
    d_inner = y_ref.shape[1]
    gw = d_inner // G_M
    act = _conv_step(xbc_ref[...], cs_ref, cs_out_ref, cw_ref, cb_ref)
    z = z_ref[...]
    dt = _softplus(dt_ref[...] + dtb_ref[...])
    dec = jnp.exp(dt * aneg_ref[...])
    se = _mm(_pack3_lanes(jnp.concatenate([dt, dec], axis=0), d_inner // P_M), e_ref[...])
    xs = act[:, :d_inner]
    xdt = xs * se[0:bb]
    dec_e = se[bb:2 * bb]
    d3 = jnp.concatenate(_split3(dec_e), axis=0)
    r3, c3 = _iota2((3 * bb, bb * N_M))
    seq3 = c3 // N_M
    onehot3 = jnp.where((r3 == seq3) | (r3 == seq3 + bb) | (r3 == seq3 + 2 * bb), 1.0, 0.0).astype(bf16)
    rb, cb_i = _iota2((bb, bb * N_M))
    own = rb == cb_i // N_M
    rowid = rb[:, 0:1]

    for g in range(G_M):
        gs = slice(g * gw, (g + 1) * gw)
        bm = act[:, d_inner + g * N_M:d_inner + (g + 1) * N_M]
        cm = act[:, d_inner + (G_M + g) * N_M:d_inner + (G_M + g + 1) * N_M].astype(bf16)
        bdiag = jnp.where(own, jnp.concatenate([bm] * bb, axis=1), 0.0).astype(bf16)
        dcol = _tn(d3[:, gs], onehot3)
        outer = _tn(xdt[:, gs].astype(bf16), bdiag)
        yg = None
        for b in range(bb):
            bs = slice(b * N_M, (b + 1) * N_M)
            s_new = st_ref[b, gs, :] * dcol[:, bs] + outer[:, bs]
            st_out_ref[b, gs, :] = s_new
            yb = _nt(cm, s_new.astype(bf16))
            yg = yb if yg is None else jnp.where(rowid == b, yb, yg)
        yacc_ref[:, gs] = yg

    y = yacc_ref[...] + xs * dexp_ref[...]
    y = y * _silu(z)
    for g in range(G_M):
        gs = slice(g * gw, (g + 1) * gw)
        y_ref[:, gs] = _rms(y[:, gs], nw_ref[:, gs]).astype(y_ref.dtype)


def _ssd_sample(projs, conv_state, state, cw, cb, dtb, aneg, dexp, nw, e_mat):
    bsz = projs[0].shape[0]
    d_inner = dexp.shape[1]
    conv_w = cw.shape[1]
    bb = SAMPLE_BB
    const = lambda a: pl.BlockSpec(a.shape, lambda i: (0, 0))
    return pl.pallas_call(
        _ssd_step_kernel,
        grid=(bsz // bb,),
        in_specs=[pl.BlockSpec((bb, p.shape[1]), lambda i: (i, 0)) for p in projs] + [
            pl.BlockSpec((CONV_K - 1, bb, conv_w), lambda i: (0, i, 0)),
            pl.BlockSpec((bb, d_inner, N_M), lambda i: (i, 0, 0)),
        ] + [const(a) for a in (cw, cb, dtb, aneg, dexp, nw, e_mat)],
        out_specs=[
            pl.BlockSpec((bb, d_inner), lambda i: (i, 0)),
            pl.BlockSpec((CONV_K - 1, bb, conv_w), lambda i: (0, i, 0)),
            pl.BlockSpec((bb, d_inner, N_M), lambda i: (i, 0, 0)),
        ],
        out_shape=[
            jax.ShapeDtypeStruct((bsz, d_inner), bf16),
            jax.ShapeDtypeStruct((CONV_K - 1, bsz, conv_w), f32),
            jax.ShapeDtypeStruct((bsz, d_inner, N_M), f32),
        ],
        scratch_shapes=[pltpu.VMEM((bb, d_inner), f32)],
        compiler_params=pltpu.CompilerParams(
            dimension_semantics=("parallel",), vmem_limit_bytes=VMEM_LIMIT),
        name="ssd_sample",
    )(*projs, conv_state, state, cw, cb, dtb, aneg, dexp, nw, e_mat)


def _gdn_step_kernel(qkv_ref, gate_ref, ba_ref, cs_ref, st_ref, cw_ref, dtb_ref, aneg_ref, nw_ref,
                     o_ref, cs_out_ref, st_out_ref):
    bb = qkv_ref.shape[0]
    n_heads = st_ref.shape[1]
    hk = n_heads * DK
    act = _conv_step(qkv_ref[...], cs_ref, cs_out_ref, cw_ref, None)
    beta = jax.nn.sigmoid(ba_ref[:, :LANES])
    eg = jnp.exp(aneg_ref[...] * _softplus(ba_ref[:, LANES:] + dtb_ref[...]))
    rb, cb_i = _iota2((bb, bb * DV))
    own = rb == cb_i // DV
    rowid = rb[:, 0:1]

    for h in range(n_heads):
        q = act[:, h * DK:(h + 1) * DK]
        k = act[:, hk + h * DK:hk + (h + 1) * DK]
        v = act[:, 2 * hk + h * DV:2 * hk + (h + 1) * DV]
        qn = q * lax.rsqrt(jnp.sum(q * q, axis=-1, keepdims=True) + EPS) * (DK ** -0.5)
        kn = k * lax.rsqrt(jnp.sum(k * k, axis=-1, keepdims=True) + EPS)
        bcol = beta[:, h:h + 1]
        ecol = eg[:, h:h + 1]
        kb = kn * bcol
        u = v * bcol
        qk = jnp.sum(qn.astype(bf16).astype(f32) * kn.astype(bf16).astype(f32), axis=-1, keepdims=True)
        wq = jnp.concatenate([(kb * ecol).astype(bf16), (qn * ecol).astype(bf16)], axis=0)
        vn = o = None
        for b in range(bb):
            ws = _mm(wq, st_ref[b, h].astype(bf16))
            vn_b = u - ws[:bb]
            o_b = ws[bb:] + qk * vn_b
            vn = vn_b if vn is None else jnp.where(rowid == b, vn_b, vn)
            o = o_b if o is None else jnp.where(rowid == b, o_b, o)
        vdiag = jnp.where(own, jnp.concatenate([vn] * bb, axis=1), 0.0).astype(bf16)
        outer = _tn(kn.astype(bf16), vdiag)
        for b in range(bb):
            st_out_ref[b, h] = st_ref[b, h] * ecol[b:b + 1, :] + outer[:, b * DV:(b + 1) * DV]
        gt = gate_ref[:, h * DV:(h + 1) * DV]
        o_ref[:, h * DV:(h + 1) * DV] = (_rms(o, nw_ref[...]) * _silu(gt)).astype(o_ref.dtype)


def _gdn_sample(projs, conv_state, state, cw, dtb, aneg, nw):
    bsz = projs[0].shape[0]
    n_heads = state.shape[1]
    conv_w = cw.shape[1]
    hv = n_heads * DV
    bb = SAMPLE_BB_GDN
    const = lambda a: pl.BlockSpec(a.shape, lambda i: (0, 0))
    return pl.pallas_call(
        _gdn_step_kernel,
        grid=(bsz // bb,),
        in_specs=[pl.BlockSpec((bb, p.shape[1]), lambda i: (i, 0)) for p in projs] + [
            pl.BlockSpec((CONV_K - 1, bb, conv_w), lambda i: (0, i, 0)),
            pl.BlockSpec((bb, n_heads, DK, DV), lambda i: (i, 0, 0, 0)),
        ] + [const(a) for a in (cw, dtb, aneg, nw)],
        out_specs=[
            pl.BlockSpec((bb, hv), lambda i: (i, 0)),
            pl.BlockSpec((CONV_K - 1, bb, conv_w), lambda i: (0, i, 0)),
            pl.BlockSpec((bb, n_heads, DK, DV), lambda i: (i, 0, 0, 0)),
        ],
        out_shape=[
            jax.ShapeDtypeStruct((bsz, hv), bf16),
            jax.ShapeDtypeStruct((CONV_K - 1, bsz, conv_w), f32),
            jax.ShapeDtypeStruct((bsz, n_heads, DK, DV), f32),
        ],
        compiler_params=pltpu.CompilerParams(
            dimension_semantics=("parallel",), vmem_limit_bytes=VMEM_LIMIT),
        name="gdn_sample",
    )(*projs, conv_state, state, cw, dtb, aneg, nw)


def _tail_kernel(x_ref, y_ref, o_ref, p_ref, nmix_ref, wmg_ref, wbs_ref, wbg_ref, wout_ref, nf_ref, wfi_ref,
                 wfo_ref, npl_ref, wpg_ref, wpp_ref, nfin_ref, out_ref, *, ff_chunks, final_norm):
    d = x_ref.shape[1]
    d_ff = wfo_ref.shape[0]
    x = x_ref[...]
    mg = _nt(_rms(x, nmix_ref[...]).astype(bf16), wmg_ref[...])
    mix = (jax.nn.sigmoid(mg[:, :d]) * _mm(y_ref[...], wbs_ref[...])
           + jax.nn.sigmoid(mg[:, d:]) * _mm(o_ref[...], wbg_ref[...]))
    x = x + _mm(mix.astype(bf16), wout_ref[...])
    h = _rms(x, nf_ref[...]).astype(bf16)
    tiles = d_ff // MXU_DIM
    bounds = [MXU_DIM * ((tiles * c + ff_chunks - 1) // ff_chunks) for c in range(ff_chunks)] + [d_ff]
    for lo, hi in zip(bounds[:-1], bounds[1:]):
        gt = _mm(h, wfi_ref[:, lo:hi])
        up = _mm(h, wfi_ref[:, d_ff + lo:d_ff + hi])
        x = x + _mm((_silu(gt) * up).astype(bf16), wfo_ref[lo:hi, :])
    pe = _mm(p_ref[...].astype(bf16), wpp_ref[...])
    x = x + pe * jax.nn.sigmoid(_mm(_rms(x, npl_ref[...]).astype(bf16), wpg_ref[...]))
    if final_norm:
        x = _rms(x, nfin_ref[...])
    out_ref[...] = x


def _tail(x, y, o, p, wts, tm, final_norm):
    t, d = x.shape
    tok = lambda w: pl.BlockSpec((tm, w), lambda i: (i, 0))
    res = lambda a: pl.BlockSpec(a.shape, lambda i: (0, 0), pipeline_mode=pl.Buffered(1))
    names = ("nmix", "wmg", "wbs", "wbg", "wout", "nf", "wfi", "wfo", "npl", "wpg", "wpp", "nfin")
    return pl.pallas_call(
        functools.partial(_tail_kernel, ff_chunks=2, final_norm=final_norm),
        grid=(t // tm,),
        in_specs=[tok(d), tok(y.shape[1]), tok(o.shape[1]), tok(p.shape[1])] + [res(wts[n]) for n in names],
        out_specs=tok(d),
        out_shape=jax.ShapeDtypeStruct((t, d), f32),
        compiler_params=pltpu.CompilerParams(
            dimension_semantics=("parallel",), vmem_limit_bytes=VMEM_LIMIT),
        name="tail",
    )(x, y, o, p, *[wts[n] for n in names])


def _pad_lanes(v, width=LANES):
    return jnp.pad(v, ((0, 0), (0, width - v.shape[1])))


def _pad_rows(v, height=LANES):
    return jnp.pad(v, ((0, height - v.shape[0]), (0, 0)))


def _layer_weights(i, d_model, norm_mix, w_in, ssm_conv_w, ssm_conv_b, ssm_dt_bias, ssm_a_log, ssm_d, ssm_norm,
                   gdn_conv_w, gdn_dt_bias, gdn_a_log, gdn_norm, w_branch_ssm, w_branch_gdn, w_out,
                   norm_ffn, w_ffn_in, w_ffn_out, norm_pl, w_pl_gate, w_pl_proj, norm_final):
    h_m = ssm_dt_bias.shape[1]
    h_g = gdn_dt_bias.shape[1]
    d_inner = h_m * P_M
    conv_m = ssm_conv_w.shape[2]
    conv_g = gdn_conv_w.shape[2]
    sizes = (d_inner, conv_m, h_m, conv_g, h_g * DV, h_g, h_g, 2 * d_model)
    starts = [0]
    for s in sizes:
        starts.append(starts[-1] + s)
    w_t = jnp.swapaxes(w_in[i], 0, 1)
    seg = lambda j: w_t[starts[j]:starts[j + 1]].astype(bf16)
    z, xbc, dtw, qkv, gate, bw, aw, mg = (seg(j) for j in range(8))
    w_ssd = (xbc, z, _pad_rows(dtw))
    w_gdn = (qkv, gate, jnp.concatenate([_pad_rows(bw), _pad_rows(aw)], axis=0))
    head_of = jnp.arange(d_inner, dtype=jnp.int32) // P_M
    assert 3 * h_m <= LANES
    rows_e = jnp.arange(LANES, dtype=jnp.int32)[:, None]
    e_mat = ((rows_e % h_m == head_of[None, :]) & (rows_e < 3 * h_m)).astype(bf16)
    row = lambda v: v.reshape(1, -1).astype(f32)
    return dict(
        norm_mix=row(norm_mix[i]), w_ssd=w_ssd, w_gdn=w_gdn,
        ssm=dict(cw=ssm_conv_w[i], cb=row(ssm_conv_b[i]), dtb=_pad_lanes(row(ssm_dt_bias[i])),
                 aneg=_pad_lanes(-jnp.exp(row(ssm_a_log[i]))), dexp=row(jnp.repeat(ssm_d[i], P_M)),
                 nw=row(ssm_norm[i]), e_mat=e_mat),
        gdn=dict(cw=gdn_conv_w[i], dtb=_pad_lanes(row(gdn_dt_bias[i])),
                 aneg=_pad_lanes(-jnp.exp(row(gdn_a_log[i]))), nw=row(gdn_norm[i])),
        tail=dict(nmix=row(norm_mix[i]), wmg=mg,
                  wbs=w_branch_ssm[i].astype(bf16), wbg=w_branch_gdn[i].astype(bf16), wout=w_out[i].astype(bf16),
                  nf=row(norm_ffn[i]), wfi=w_ffn_in[i].astype(bf16), wfo=w_ffn_out[i].astype(bf16),
                  npl=row(norm_pl[i]), wpg=w_pl_gate[i].astype(bf16), wpp=w_pl_proj[i].astype(bf16),
                  nfin=row(norm_final)),
    )


def _pick_tile(n, candidates):
    for c in candidates:
        if n % c == 0:
            return c
    raise ValueError(f"no tile for {n}")


def kernel(x_prompt, x_sample, p_prompt, p_sample, state_ssm, state_ssm_conv, state_gdn, state_gdn_conv,
           norm_mix, w_in, ssm_conv_w, ssm_conv_b, ssm_dt_bias, ssm_a_log, ssm_d, ssm_norm,
           gdn_conv_w, gdn_dt_bias, gdn_a_log, gdn_norm, w_branch_ssm, w_branch_gdn, w_out,
           norm_ffn, w_ffn_in, w_ffn_out, norm_pl, w_pl_gate, w_pl_proj, norm_final):
    depth = p_prompt.shape[0]
    bp, seq, d_model = x_prompt.shape
    bs, dec_seq, _ = x_sample.shape
    assert dec_seq == 1 and seq % STEP_TOKENS == 0 and bs % SAMPLE_BB == 0 and bs % SAMPLE_BB_GDN == 0
    h_m, h_g = ssm_dt_bias.shape[1], gdn_dt_bias.shape[1]
    d_inner = h_m * P_M
    tail_rows = slice(SUBLANES - (CONV_K - 1), SUBLANES)

    xp = x_prompt.reshape(bp * seq, d_model)
    xs = x_sample.reshape(bs, d_model)
    new_p = ([], [], [], [])
    new_s = ([], [], [], [])
    for i in range(depth):
        lw = _layer_weights(i, d_model, norm_mix, w_in, ssm_conv_w, ssm_conv_b, ssm_dt_bias, ssm_a_log, ssm_d,
                            ssm_norm, gdn_conv_w, gdn_dt_bias, gdn_a_log, gdn_norm, w_branch_ssm, w_branch_gdn,
                            w_out, norm_ffn, w_ffn_in, w_ffn_out, norm_pl, w_pl_gate, w_pl_proj, norm_final)
        last = i == depth - 1

        tp = bp * seq
        y, st_ssm, cs_ssm = _ssd_prompt(xp, lw["norm_mix"], lw["w_ssd"], bp, seq, **lw["ssm"])
        o, st_gdn, cs_gdn = _gdn_prompt(xp, lw["norm_mix"], lw["w_gdn"], bp, seq, n_heads=h_g, **lw["gdn"])
        xp = _tail(xp, y, o, p_prompt[i].reshape(tp, -1), lw["tail"], _pick_tile(tp, (256, 128)), last)
        new_p[0].append(st_ssm.reshape(bp, h_m, P_M, N_M))
        new_p[1].append(cs_ssm[:, tail_rows, :])
        new_p[2].append(st_gdn)
        new_p[3].append(cs_gdn[:, tail_rows, :])

        ys, cs_s, ss_s = _ssd_sample(_inproj(xs, lw["norm_mix"], lw["w_ssd"]),
                                     jnp.swapaxes(state_ssm_conv[i], 0, 1),
                                     state_ssm[i].reshape(bs, d_inner, N_M), **lw["ssm"])
        os_, cs_g, ss_g = _gdn_sample(_inproj(xs, lw["norm_mix"], lw["w_gdn"]),
                                      jnp.swapaxes(state_gdn_conv[i], 0, 1), state_gdn[i], **lw["gdn"])
        xs = _tail(xs, ys, os_, p_sample[i].reshape(bs, -1), lw["tail"],
                   _pick_tile(bs, (128, 64, 32, 16, 8)), last)
        new_s[0].append(ss_s.reshape(bs, h_m, P_M, N_M))
        new_s[1].append(jnp.swapaxes(cs_s, 0, 1))
        new_s[2].append(ss_g)
        new_s[3].append(jnp.swapaxes(cs_g, 0, 1))

    stack = lambda lst: jnp.stack(lst)
    return (xp.reshape(bp, seq, d_model), xs.reshape(bs, 1, d_model),
            stack(new_p[0]), stack(new_p[1]), stack(new_p[2]), stack(new_p[3]),
            stack(new_s[0]), stack(new_s[1]), stack(new_s[2]), stack(new_s[3]))
```

```python
import functools

import jax
import jax.numpy as jnp
from jax import lax
from jax.experimental import pallas as pl
from jax.experimental.pallas import tpu as pltpu

f32 = jnp.float32
bf16 = jnp.bfloat16

EPS = 1e-6
CONV_K = 4
LANES = 128
SUBLANES = 8
MXU_DIM = 256
VMEM_LIMIT = 56 * 1024 * 1024

P_M = 64
N_M = 128
G_M = 4
DK = 128
DV = 128
SSD_BLOCK = 128
GDN_CHUNK = 64
GDN_BLOCK = 128
GDN_BASE = 16
STEP_TOKENS = 512
CONV_COLS = 512
SAMPLE_BB = 8
SAMPLE_BB_GDN = 16


def _nt(a, b):
    return lax.dot_general(a, b, (((1,), (1,)), ((), ())), preferred_element_type=f32)


def _tn(a, b):
    return lax.dot_general(a, b, (((0,), (0,)), ((), ())), preferred_element_type=f32)


def _mm(a, b):
    return jnp.dot(a, b, preferred_element_type=f32)


def _split3(x):
    hi = x.astype(bf16)
    r1 = x - hi.astype(f32)
    mid = r1.astype(bf16)
    lo = (r1 - mid.astype(f32)).astype(bf16)
    return hi, mid, lo


def _mm_sel_rhs(x, sel3):
    return _mm(jnp.concatenate(_split3(x), axis=1), sel3)


def _mm_sel_lhs(sel, x):
    return _mm(jnp.concatenate([sel] * 3, axis=1), jnp.concatenate(_split3(x), axis=0))


def _pack3_lanes(x, width):
    lane = lax.broadcasted_iota(jnp.int32, x.shape, 1)
    hi, mid, lo = (t.astype(f32) for t in _split3(jnp.where(lane < width, x, 0.0)))
    return (hi + pltpu.roll(mid, width, 1) + pltpu.roll(lo, 2 * width, 1)).astype(bf16)


def _silu(x):
    h = 0.5 * x
    return h + h * jnp.tanh(h)


def _softplus(x):
    return jnp.maximum(x, 0.0) + jnp.log1p(jnp.exp(-jnp.abs(x)))


def _rms(x, g):
    return x * lax.rsqrt(jnp.mean(x * x, axis=-1, keepdims=True) + EPS) * g


def _iota2(shape):
    return lax.broadcasted_iota(jnp.int32, shape, 0), lax.broadcasted_iota(jnp.int32, shape, 1)


def _inproj_kernel(x_ref, g_ref, *refs):
    n = len(refs) // 2
    h = _rms(x_ref[...], g_ref[...]).astype(bf16)
    for w_ref, o_ref in zip(refs[:n], refs[n:]):
        o_ref[...] = _nt(h, w_ref[...])


def _inproj(x, g, ws):
    t, d = x.shape
    full = lambda shape: pl.BlockSpec(shape, lambda i: (0, 0))
    return pl.pallas_call(
        _inproj_kernel,
        grid=(1,),
        in_specs=[full((t, d)), full((1, d))] + [full(w.shape) for w in ws],
        out_specs=[full((t, w.shape[0])) for w in ws],
        out_shape=[jax.ShapeDtypeStruct((t, w.shape[0]), f32) for w in ws],
        compiler_params=pltpu.CompilerParams(dimension_semantics=("arbitrary",), vmem_limit_bytes=VMEM_LIMIT),
        name="inproj",
    )(x, g, *ws)


def _project_conv(h, w_ref, ext_ref, act_ref, cw_ref, cb_ref, rows, width):
    for c0 in range(0, width, CONV_COLS):
        sl = slice(c0, c0 + CONV_COLS)
        u = _nt(h, w_ref[sl, :])
        ext_ref[SUBLANES:SUBLANES + rows, sl] = u
        cw = cw_ref[:, sl]
        conv = u * cw[3:4]
        full = ext_ref[:, sl]
        for j in range(CONV_K - 1):
            shifted = pltpu.roll(full, CONV_K - 1 - j, 0)
            conv = conv + shifted[SUBLANES:SUBLANES + rows] * cw[j:j + 1]
        if cb_ref is not None:
            conv = conv + cb_ref[:, sl]
        act_ref[:, sl] = _silu(conv)
        ext_ref[0:SUBLANES, sl] = ext_ref[rows:rows + SUBLANES, sl]


def _ssd_kernel(x_ref, g_ref, wx_ref, wz_ref, wdt_ref, cw_ref, cb_ref, dtb_ref, aneg_ref, dexp_ref, nw_ref, e_ref,
                y_ref, st_out_ref, cs_out_ref, ext_ref, act_ref, z_ref, st_ref, yd_ref):
    i = pl.program_id(1)
    rows = x_ref.shape[0]
    c = SSD_BLOCK
    conv_w = act_ref.shape[1]
    d_inner = y_ref.shape[1]
    gw = d_inner // G_M
    hg = gw // P_M
    n_heads = d_inner // P_M

    @pl.when(i == 0)
    def _():
        ext_ref[0:SUBLANES, :] = jnp.zeros((SUBLANES, conv_w), f32)
        st_ref[...] = jnp.zeros(st_ref.shape, f32)

    h = _rms(x_ref[...], g_ref[...]).astype(bf16)
    _project_conv(h, wx_ref, ext_ref, act_ref, cw_ref, cb_ref, rows, conv_w)
    dt_raw = _nt(h, wdt_ref[...])
    for g in range(G_M):
        z_ref[:, g * gw:(g + 1) * gw] = _nt(h, wz_ref[g * gw:(g + 1) * gw, :])

    r, cc = _iota2((c, c))
    lower = r >= cc
    lower_b = jnp.where(lower, 1.0, 0.0).astype(bf16)
    upper3 = jnp.concatenate([jnp.where(r <= cc, 1.0, 0.0).astype(bf16)] * 3, axis=0)

    for sub in range(rows // c):
        rs = slice(sub * c, (sub + 1) * c)
        dt = _softplus(dt_raw[rs] + dtb_ref[...])
        da = dt * aneg_ref[...]
        acum = _mm_sel_lhs(lower_b, da)
        acum_t = _mm_sel_rhs(da.T, upper3)
        alast = acum[c - 1:c, :]
        dout = jnp.exp(alast - acum)
        scal = _pack3_lanes(jnp.concatenate(
            [dt, dt * dout, jnp.exp(acum), jnp.broadcast_to(jnp.exp(alast), (SUBLANES, LANES))], axis=0), n_heads)
        for g in range(G_M):
            gs = slice(g * gw, (g + 1) * gw)
            se = _mm(scal, e_ref[:, gs])
            dt_e, dd_e, ea_e, cd_e = se[0:c], se[c:2 * c], se[2 * c:3 * c], se[3 * c:3 * c + 1]
            xs = act_ref[rs, gs]
            bm = act_ref[rs, d_inner + g * N_M:d_inner + (g + 1) * N_M].astype(bf16)
            cm = act_ref[rs, d_inner + (G_M + g) * N_M:d_inner + (G_M + g + 1) * N_M].astype(bf16)
            xdt = (xs * dt_e).astype(bf16)
            cb = _nt(cm, bm).astype(bf16)
            st = st_ref[:, gs]
            y_off = _mm(cm, st.astype(bf16)) * ea_e
            st_ref[:, gs] = st * cd_e + _tn(bm, (xs * dd_e).astype(bf16))
            for hh in range(hg):
                hd = g * hg + hh
                lmat = jnp.exp(jnp.where(lower, acum[:, hd:hd + 1] - acum_t[hd:hd + 1, :], -jnp.inf))
                yd_ref[:, hh * P_M:(hh + 1) * P_M] = _mm(cb * lmat.astype(bf16), xdt[:, hh * P_M:(hh + 1) * P_M])
            y = yd_ref[...] + y_off + xs * dexp_ref[:, gs]
            y = y * _silu(z_ref[rs, gs])
            y_ref[rs, gs] = _rms(y, nw_ref[:, gs]).astype(y_ref.dtype)

    @pl.when(i == pl.num_programs(1) - 1)
    def _():
        cs_out_ref[0] = ext_ref[0:SUBLANES, :]
        for j in range(d_inner // LANES):
            st_out_ref[0, j * LANES:(j + 1) * LANES, :] = st_ref[:, j * LANES:(j + 1) * LANES].T


def _ssd_prompt(x, g, ws, bsz, seq, cw, cb, dtb, aneg, dexp, nw, e_mat):
    d_model = x.shape[1]
    d_inner = dexp.shape[1]
    conv_w = cw.shape[1]
    rows = STEP_TOKENS
    nb = seq // rows
    tok = lambda b, i: (b * nb + i, 0)
    res = lambda a: pl.BlockSpec(a.shape, lambda b, i: (0, 0), pipeline_mode=pl.Buffered(1))
    return pl.pallas_call(
        _ssd_kernel,
        grid=(bsz, nb),
        in_specs=[pl.BlockSpec((rows, d_model), tok)] + [res(a) for a in (g, *ws, cw, cb, dtb, aneg, dexp, nw, e_mat)],
        out_specs=[
            pl.BlockSpec((rows, d_inner), tok),
            pl.BlockSpec((1, d_inner, N_M), lambda b, i: (b, 0, 0)),
            pl.BlockSpec((1, SUBLANES, conv_w), lambda b, i: (b, 0, 0)),
        ],
        out_shape=[
            jax.ShapeDtypeStruct((bsz * seq, d_inner), bf16),
            jax.ShapeDtypeStruct((bsz, d_inner, N_M), f32),
            jax.ShapeDtypeStruct((bsz, SUBLANES, conv_w), f32),
        ],
        scratch_shapes=[
            pltpu.VMEM((rows + SUBLANES, conv_w), f32),
            pltpu.VMEM((rows, conv_w), f32),
            pltpu.VMEM((rows, d_inner), f32),
            pltpu.VMEM((N_M, d_inner), f32),
            pltpu.VMEM((SSD_BLOCK, d_inner // G_M), f32),
        ],
        compiler_params=pltpu.CompilerParams(
            dimension_semantics=("parallel", "arbitrary"), vmem_limit_bytes=VMEM_LIMIT),
        name="ssd_prompt",
    )(x, g, *ws, cw, cb, dtb, aneg, dexp, nw, e_mat)


def _gdn_kernel(x_ref, g_ref, wx_ref, wgate_ref, wba_ref, cw_ref, dtb_ref, aneg_ref, nw_ref,
                o_ref, st_out_ref, cs_out_ref, ext_ref, act_ref, gate_ref, st_ref):
    i = pl.program_id(1)
    rows = x_ref.shape[0]
    blk = GDN_BLOCK
    ck = GDN_CHUNK
    conv_w = act_ref.shape[1]
    n_heads = st_ref.shape[0]
    hk = n_heads * DK
    per_blk = blk // ck
    heads = range(n_heads)

    @pl.when(i == 0)
    def _():
        ext_ref[0:SUBLANES, :] = jnp.zeros((SUBLANES, conv_w), f32)
        st_ref[...] = jnp.zeros(st_ref.shape, f32)

    h_in = _rms(x_ref[...], g_ref[...]).astype(bf16)
    _project_conv(h_in, wx_ref, ext_ref, act_ref, cw_ref, None, rows, conv_w)
    ba = _nt(h_in, wba_ref[...])
    gate_ref[...] = _nt(h_in, wgate_ref[...])

    r, cc = _iota2((blk, blk))
    same = (r // ck) == (cc // ck)
    lower = same & (r >= cc)
    strict = same & (r > cc)
    lower_b = jnp.where(lower, 1.0, 0.0).astype(bf16)
    upper3 = jnp.concatenate([jnp.where(same & (r <= cc), 1.0, 0.0).astype(bf16)] * 3, axis=0)
    rcol = r[:, 0:1]
    eye = jnp.where(r == cc, 1.0, 0.0)
    blk_masks = []
    size = GDN_BASE
    while size <= ck:
        blk_masks.append((r // size) == (cc // size))
        size *= 2

    n_sub = rows // blk
    prep = [None] * n_sub
    xs = [None] * n_sub
    qks, qgs, kgs, gls = ([None] * (n_sub * n_heads) for _ in range(4))

    def prepare(sub):
        rs = slice(sub * blk, (sub + 1) * blk)
        beta = jax.nn.sigmoid(ba[rs, :LANES])
        glog = aneg_ref[...] * _softplus(ba[rs, LANES:] + dtb_ref[...])
        gcum = _mm_sel_lhs(lower_b, glog)
        gcum_t = _mm_sel_rhs(glog.T, upper3)
        nmats, rhs = [], []
        for h in heads:
            q = act_ref[rs, h * DK:(h + 1) * DK]
            k = act_ref[rs, hk + h * DK:hk + (h + 1) * DK]
            v = act_ref[rs, 2 * hk + h * DV:2 * hk + (h + 1) * DV]
            qn = q * lax.rsqrt(jnp.sum(q * q, axis=-1, keepdims=True) + EPS) * (DK ** -0.5)
            kn = k * lax.rsqrt(jnp.sum(k * k, axis=-1, keepdims=True) + EPS)
            bcol = beta[:, h:h + 1]
            gcol = gcum[:, h:h + 1]
            eg = jnp.exp(gcol)
            dm = jnp.exp(jnp.where(lower, gcol - gcum_t[h:h + 1, :], -jnp.inf))
            knb = kn.astype(bf16)
            kb = kn * bcol
            nmats.append(jnp.where(strict, -(_nt(kb.astype(bf16), knb) * dm), 0.0))
            rhs.append(jnp.concatenate([v * bcol, kb * eg], axis=-1))
            glast = [gcol[(j + 1) * ck - 1:(j + 1) * ck] for j in range(per_blk)]
            glast_row = glast[per_blk - 1]
            for j in range(per_blk - 2, -1, -1):
                glast_row = jnp.where(rcol < (j + 1) * ck, glast[j], glast_row)
            idx = sub * n_heads + h
            qks[idx] = jnp.where(lower, _nt(qn.astype(bf16), knb) * dm, 0.0).astype(bf16)
            qgs[idx] = (qn * eg).astype(bf16)
            kgs[idx] = (kn * jnp.exp(glast_row - gcol)).astype(bf16)
            gls[idx] = [jnp.exp(gl) for gl in glast]
            yield
        prep[sub] = (nmats, rhs)

    def solve(sub):
        nmats, rhs = prep[sub]
        qs = [jnp.where(blk_masks[0], n, 0.0) for n in nmats]
        minv = [eye + q for q in qs]
        span = 2
        while span < GDN_BASE:
            qbs = [q.astype(bf16) for q in qs]
            qs = [_mm(qb, qb) for qb in qbs]
            yield
            minv = [m + _mm(m.astype(bf16), q.astype(bf16)) for m, q in zip(minv, qs)]
            yield
            span *= 2
        for lvl in range(1, len(blk_masks)):
            off = blk_masks[lvl] & jnp.logical_not(blk_masks[lvl - 1])
            mbs = [m.astype(bf16) for m in minv]
            ems = [_mm(jnp.where(off, n, 0.0).astype(bf16), mb) for n, mb in zip(nmats, mbs)]
            yield
            minv = [m + _mm(mb, em.astype(bf16)) for m, mb, em in zip(minv, mbs, ems)]
            yield
        xs[sub] = [_mm(m.astype(bf16), x.astype(bf16)) for m, x in zip(minv, rhs)]

    state = {"s": [st_ref[h] for h in heads]}
    zeros = jnp.zeros((ck, DV), bf16)

    def recur(sub):
        for j in range(per_blk):
            rl = slice(j * ck, (j + 1) * ck)
            ro = slice(sub * blk + j * ck, sub * blk + (j + 1) * ck)
            at = lambda lst, h: lst[sub * n_heads + h]
            ss = state["s"]
            sbs = [s.astype(bf16) for s in ss]
            vns = [xs[sub][h][rl, :DV] - _mm(xs[sub][h][rl, DV:].astype(bf16), sbs[h]) for h in heads]
            yield
            vfull = [jnp.concatenate([zeros] * j + [vns[h].astype(bf16)] + [zeros] * (per_blk - 1 - j), axis=0)
                     for h in heads]
            outs = [_mm(at(qgs, h)[rl], sbs[h]) + _mm(at(qks, h)[rl, :], vfull[h]) for h in heads]
            yield
            state["s"] = [ss[h] * at(gls, h)[j] + _tn(at(kgs, h)[rl], vns[h].astype(bf16)) for h in heads]
            yield
            for h in heads:
                gt = gate_ref[ro, h * DV:(h + 1) * DV]
                o_ref[ro, h * DV:(h + 1) * DV] = (_rms(outs[h], nw_ref[...]) * _silu(gt)).astype(o_ref.dtype)
            yield

    def emit(*gens):
        live = list(gens)
        while live:
            for gen in list(live):
                if next(gen, StopIteration) is StopIteration:
                    live.remove(gen)

    emit(prepare(0))
    for sub in range(n_sub):
        later = [prepare(sub + 1)] if sub + 1 < n_sub else []
        earlier = [recur(sub - 1)] if sub > 0 else []
        emit(solve(sub), *later, *earlier)
    emit(recur(n_sub - 1))
    for h in heads:
        st_ref[h] = state["s"][h]

    @pl.when(i == pl.num_programs(1) - 1)
    def _():
        cs_out_ref[0] = ext_ref[0:SUBLANES, :]
        st_out_ref[0] = st_ref[...]


def _gdn_prompt(x, g, ws, bsz, seq, cw, dtb, aneg, nw, n_heads):
    d_model = x.shape[1]
    conv_w = cw.shape[1]
    hv = n_heads * DV
    rows = STEP_TOKENS
    nb = seq // rows
    tok = lambda b, i: (b * nb + i, 0)
    res = lambda a: pl.BlockSpec(a.shape, lambda b, i: (0, 0), pipeline_mode=pl.Buffered(1))
    return pl.pallas_call(
        _gdn_kernel,
        grid=(bsz, nb),
        in_specs=[pl.BlockSpec((rows, d_model), tok)] + [res(a) for a in (g, *ws, cw, dtb, aneg, nw)],
        out_specs=[
            pl.BlockSpec((rows, hv), tok),
            pl.BlockSpec((1, n_heads, DK, DV), lambda b, i: (b, 0, 0, 0)),
            pl.BlockSpec((1, SUBLANES, conv_w), lambda b, i: (b, 0, 0)),
        ],
        out_shape=[
            jax.ShapeDtypeStruct((bsz * seq, hv), bf16),
            jax.ShapeDtypeStruct((bsz, n_heads, DK, DV), f32),
            jax.ShapeDtypeStruct((bsz, SUBLANES, conv_w), f32),
        ],
        scratch_shapes=[
            pltpu.VMEM((rows + SUBLANES, conv_w), f32),
            pltpu.VMEM((rows, conv_w), f32),
            pltpu.VMEM((rows, hv), f32),
            pltpu.VMEM((n_heads, DK, DV), f32),
        ],
        compiler_params=pltpu.CompilerParams(
            dimension_semantics=("parallel", "arbitrary"), vmem_limit_bytes=VMEM_LIMIT),
        name="gdn_prompt",
    )(x, g, *ws, cw, dtb, aneg, nw)


def _conv_step(u, cs_ref, cs_out_ref, cw_ref, cb_ref):
    cw = cw_ref[...]
    conv = u * cw[3:4]
    for j in range(CONV_K - 1):
        conv = conv + cs_ref[j] * cw[j:j + 1]
    if cb_ref is not None:
        conv = conv + cb_ref[...]
    for j in range(CONV_K - 2):
        cs_out_ref[j] = cs_ref[j + 1]
    cs_out_ref[CONV_K - 2] = u
    return _silu(conv)


def _ssd_step_kernel(xbc_ref, z_ref, dt_ref, cs_ref, st_ref, cw_ref, cb_ref, dtb_ref, aneg_ref, dexp_ref,
                     nw_ref, e_ref, y_ref, cs_out_ref, st_out_ref, yacc_ref):
    bb = xbc_ref.shape[0]
    d_inner = y_ref.shape[1]
    gw = d_inner // G_M
    act = _conv_step(xbc_ref[...], cs_ref, cs_out_ref, cw_ref, cb_ref)
    z = z_ref[...]
    dt = _softplus(dt_ref[...] + dtb_ref[...])
    dec = jnp.exp(dt * aneg_ref[...])
    se = _mm(_pack3_lanes(jnp.concatenate([dt, dec], axis=0), d_inner // P_M), e_ref[...])
    xs = act[:, :d_inner]
    xdt = xs * se[0:bb]
    dec_e = se[bb:2 * bb]
    d3 = jnp.concatenate(_split3(dec_e), axis=0)
    r3, c3 = _iota2((3 * bb, bb * N_M))
    seq3 = c3 // N_M
    onehot3 = jnp.where((r3 == seq3) | (r3 == seq3 + bb) | (r3 == seq3 + 2 * bb), 1.0, 0.0).astype(bf16)
    rb, cb_i = _iota2((bb, bb * N_M))
    own = rb == cb_i // N_M
    rowid = rb[:, 0:1]

    for g in range(G_M):
        gs = slice(g * gw, (g + 1) * gw)
        bm = act[:, d_inner + g * N_M:d_inner + (g + 1) * N_M]
        cm = act[:, d_inner + (G_M + g) * N_M:d_inner + (G_M + g + 1) * N_M].astype(bf16)
        bdiag = jnp.where(own, jnp.concatenate([bm] * bb, axis=1), 0.0).astype(bf16)
        dcol = _tn(d3[:, gs], onehot3)
        outer = _tn(xdt[:, gs].astype(bf16), bdiag)
        yg = None
        for b in range(bb):
            bs = slice(b * N_M, (b + 1) * N_M)
            s_new = st_ref[b, gs, :] * dcol[:, bs] + outer[:, bs]
            st_out_ref[b, gs, :] = s_new
            yb = _nt(cm, s_new.astype(bf16))
            yg = yb if yg is None else jnp.where(rowid == b, yb, yg)
        yacc_ref[:, gs] = yg

    y = yacc_ref[...] + xs * dexp_ref[...]
    y = y * _silu(z)
    for g in range(G_M):
        gs = slice(g * gw, (g + 1) * gw)
        y_ref[:, gs] = _rms(y[:, gs], nw_ref[:, gs]).astype(y_ref.dtype)


def _ssd_sample(projs, conv_state, state, cw, cb, dtb, aneg, dexp, nw, e_mat):
    bsz = projs[0].shape[0]
    d_inner = dexp.shape[1]
    conv_w = cw.shape[1]
    bb = SAMPLE_BB
    const = lambda a: pl.BlockSpec(a.shape, lambda i: (0, 0))
    return pl.pallas_call(
        _ssd_step_kernel,
        grid=(bsz // bb,),
        in_specs=[pl.BlockSpec((bb, p.shape[1]), lambda i: (i, 0)) for p in projs] + [
            pl.BlockSpec((CONV_K - 1, bb, conv_w), lambda i: (0, i, 0)),
            pl.BlockSpec((bb, d_inner, N_M), lambda i: (i, 0, 0)),
        ] + [const(a) for a in (cw, cb, dtb, aneg, dexp, nw, e_mat)],
        out_specs=[
            pl.BlockSpec((bb, d_inner), lambda i: (i, 0)),
            pl.BlockSpec((CONV_K - 1, bb, conv_w), lambda i: (0, i, 0)),
            pl.BlockSpec((bb, d_inner, N_M), lambda i: (i, 0, 0)),
        ],
        out_shape=[
            jax.ShapeDtypeStruct((bsz, d_inner), bf16),
            jax.ShapeDtypeStruct((CONV_K - 1, bsz, conv_w), f32),
            jax.ShapeDtypeStruct((bsz, d_inner, N_M), f32),
        ],
        scratch_shapes=[pltpu.VMEM((bb, d_inner), f32)],
        compiler_params=pltpu.CompilerParams(
            dimension_semantics=("parallel",), vmem_limit_bytes=VMEM_LIMIT),
        name="ssd_sample",
    )(*projs, conv_state, state, cw, cb, dtb, aneg, dexp, nw, e_mat)


def _gdn_step_kernel(qkv_ref, gate_ref, ba_ref, cs_ref, st_ref, cw_ref, dtb_ref, aneg_ref, nw_ref,
                     o_ref, cs_out_ref, st_out_ref):
    bb = qkv_ref.shape[0]
    n_heads = st_ref.shape[1]
    hk = n_heads * DK
    act = _conv_step(qkv_ref[...], cs_ref, cs_out_ref, cw_ref, None)
    beta = jax.nn.sigmoid(ba_ref[:, :LANES])
    eg = jnp.exp(aneg_ref[...] * _softplus(ba_ref[:, LANES:] + dtb_ref[...]))
    rb, cb_i = _iota2((bb, bb * DV))
    own = rb == cb_i // DV
    rowid = rb[:, 0:1]

    for h in range(n_heads):
        q = act[:, h * DK:(h + 1) * DK]
        k = act[:, hk + h * DK:hk + (h + 1) * DK]
        v = act[:, 2 * hk + h * DV:2 * hk + (h + 1) * DV]
        qn = q * lax.rsqrt(jnp.sum(q * q, axis=-1, keepdims=True) + EPS) * (DK ** -0.5)
        kn = k * lax.rsqrt(jnp.sum(k * k, axis=-1, keepdims=True) + EPS)
        bcol = beta[:, h:h + 1]
        ecol = eg[:, h:h + 1]
        kb = kn * bcol
        u = v * bcol
        qk = jnp.sum(qn.astype(bf16).astype(f32) * kn.astype(bf16).astype(f32), axis=-1, keepdims=True)
        wq = jnp.concatenate([(kb * ecol).astype(bf16), (qn * ecol).astype(bf16)], axis=0)
        vn = o = None
        for b in range(bb):
            ws = _mm(wq, st_ref[b, h].astype(bf16))
            vn_b = u - ws[:bb]
            o_b = ws[bb:] + qk * vn_b
            vn = vn_b if vn is None else jnp.where(rowid == b, vn_b, vn)
            o = o_b if o is None else jnp.where(rowid == b, o_b, o)
        vdiag = jnp.where(own, jnp.concatenate([vn] * bb, axis=1), 0.0).astype(bf16)
        outer = _tn(kn.astype(bf16), vdiag)
        for b in range(bb):
            st_out_ref[b, h] = st_ref[b, h] * ecol[b:b + 1, :] + outer[:, b * DV:(b + 1) * DV]
        gt = gate_ref[:, h * DV:(h + 1) * DV]
        o_ref[:, h * DV:(h + 1) * DV] = (_rms(o, nw_ref[...]) * _silu(gt)).astype(o_ref.dtype)


def _gdn_sample(projs, conv_state, state, cw, dtb, aneg, nw):
    bsz = projs[0].shape[0]
    n_heads = state.shape[1]
    conv_w = cw.shape[1]
    hv = n_heads * DV
    bb = SAMPLE_BB_GDN
    const = lambda a: pl.BlockSpec(a.shape, lambda i: (0, 0))
    return pl.pallas_call(
        _gdn_step_kernel,
        grid=(bsz // bb,),
        in_specs=[pl.BlockSpec((bb, p.shape[1]), lambda i: (i, 0)) for p in projs] + [
            pl.BlockSpec((CONV_K - 1, bb, conv_w), lambda i: (0, i, 0)),
            pl.BlockSpec((bb, n_heads, DK, DV), lambda i: (i, 0, 0, 0)),
        ] + [const(a) for a in (cw, dtb, aneg, nw)],
        out_specs=[
            pl.BlockSpec((bb, hv), lambda i: (i, 0)),
            pl.BlockSpec((CONV_K - 1, bb, conv_w), lambda i: (0, i, 0)),
            pl.BlockSpec((bb, n_heads, DK, DV), lambda i: (i, 0, 0, 0)),
        ],
        out_shape=[
            jax.ShapeDtypeStruct((bsz, hv), bf16),
            jax.ShapeDtypeStruct((CONV_K - 1, bsz, conv_w), f32),
            jax.ShapeDtypeStruct((bsz, n_heads, DK, DV), f32),
        ],
        compiler_params=pltpu.CompilerParams(
            dimension_semantics=("parallel",), vmem_limit_bytes=VMEM_LIMIT),
        name="gdn_sample",
    )(*projs, conv_state, state, cw, dtb, aneg, nw)


def _tail_kernel(x_ref, y_ref, o_ref, p_ref, nmix_ref, wmg_ref, wbs_ref, wbg_ref, wout_ref, nf_ref, wfi_ref,
                 wfo_ref, npl_ref, wpg_ref, wpp_ref, nfin_ref, out_ref, *, ff_chunks, final_norm):
    d = x_ref.shape[1]
    d_ff = wfo_ref.shape[0]
    x = x_ref[...]
    ys = _mm(y_ref[...], wbs_ref[...])
    og = _mm(o_ref[...], wbg_ref[...])
    mg = _nt(_rms(x, nmix_ref[...]).astype(bf16), wmg_ref[...])
    mix = jax.nn.sigmoid(mg[:, :d]) * ys + jax.nn.sigmoid(mg[:, d:]) * og
    x = x + _mm(mix.astype(bf16), wout_ref[...])
    pe = _mm(p_ref[...].astype(bf16), wpp_ref[...])
    h = _rms(x, nf_ref[...]).astype(bf16)
    tiles = d_ff // MXU_DIM
    bounds = [MXU_DIM * ((tiles * c + ff_chunks - 1) // ff_chunks) for c in range(ff_chunks)] + [d_ff]
    for lo, hi in zip(bounds[:-1], bounds[1:]):
        gt = _mm(h, wfi_ref[:, lo:hi])
        up = _mm(h, wfi_ref[:, d_ff + lo:d_ff + hi])
        x = x + _mm((_silu(gt) * up).astype(bf16), wfo_ref[lo:hi, :])
    x = x + pe * jax.nn.sigmoid(_mm(_rms(x, npl_ref[...]).astype(bf16), wpg_ref[...]))
    if final_norm:
        x = _rms(x, nfin_ref[...])
    out_ref[...] = x


def _tail(x, y, o, p, wts, tm, final_norm):
    t, d = x.shape
    tok = lambda w: pl.BlockSpec((tm, w), lambda i: (i, 0))
    res = lambda a: pl.BlockSpec(a.shape, lambda i: (0, 0), pipeline_mode=pl.Buffered(1))
    names = ("nmix", "wmg", "wbs", "wbg", "wout", "nf", "wfi", "wfo", "npl", "wpg", "wpp", "nfin")
    return pl.pallas_call(
        functools.partial(_tail_kernel, ff_chunks=2, final_norm=final_norm),
        grid=(t // tm,),
        in_specs=[tok(d), tok(y.shape[1]), tok(o.shape[1]), tok(p.shape[1])] + [res(wts[n]) for n in names],
        out_specs=tok(d),
        out_shape=jax.ShapeDtypeStruct((t, d), f32),
        compiler_params=pltpu.CompilerParams(
            dimension_semantics=("parallel",), vmem_limit_bytes=VMEM_LIMIT),
        name="tail",
    )(x, y, o, p, *[wts[n] for n in names])


def _pad_lanes(v, width=LANES):
    return jnp.pad(v, ((0, 0), (0, width - v.shape[1])))


def _pad_rows(v, height=LANES):
    return jnp.pad(v, ((0, height - v.shape[0]), (0, 0)))


def _layer_weights(i, d_model, norm_mix, w_in, ssm_conv_w, ssm_conv_b, ssm_dt_bias, ssm_a_log, ssm_d, ssm_norm,
                   gdn_conv_w, gdn_dt_bias, gdn_a_log, gdn_norm, w_branch_ssm, w_branch_gdn, w_out,
                   norm_ffn, w_ffn_in, w_ffn_out, norm_pl, w_pl_gate, w_pl_proj, norm_final):
    h_m = ssm_dt_bias.shape[1]
    h_g = gdn_dt_bias.shape[1]
    d_inner = h_m * P_M
    conv_m = ssm_conv_w.shape[2]
    conv_g = gdn_conv_w.shape[2]
    sizes = (d_inner, conv_m, h_m, conv_g, h_g * DV, h_g, h_g, 2 * d_model)
    starts = [0]
    for s in sizes:
        starts.append(starts[-1] + s)
    w_t = jnp.swapaxes(w_in[i], 0, 1)
    seg = lambda j: w_t[starts[j]:starts[j + 1]].astype(bf16)
    z, xbc, dtw, qkv, gate, bw, aw, mg = (seg(j) for j in range(8))
    w_ssd = (xbc, z, _pad_rows(dtw))
    w_gdn = (qkv, gate, jnp.concatenate([_pad_rows(bw), _pad_rows(aw)], axis=0))
    head_of = jnp.arange(d_inner, dtype=jnp.int32) // P_M
    assert 3 * h_m <= LANES
    rows_e = jnp.arange(LANES, dtype=jnp.int32)[:, None]
    e_mat = ((rows_e % h_m == head_of[None, :]) & (rows_e < 3 * h_m)).astype(bf16)
    row = lambda v: v.reshape(1, -1).astype(f32)
    return dict(
        norm_mix=row(norm_mix[i]), w_ssd=w_ssd, w_gdn=w_gdn,
        ssm=dict(cw=ssm_conv_w[i], cb=row(ssm_conv_b[i]), dtb=_pad_lanes(row(ssm_dt_bias[i])),
                 aneg=_pad_lanes(-jnp.exp(row(ssm_a_log[i]))), dexp=row(jnp.repeat(ssm_d[i], P_M)),
                 nw=row(ssm_norm[i]), e_mat=e_mat),
        gdn=dict(cw=gdn_conv_w[i], dtb=_pad_lanes(row(gdn_dt_bias[i])),
                 aneg=_pad_lanes(-jnp.exp(row(gdn_a_log[i]))), nw=row(gdn_norm[i])),
        tail=dict(nmix=row(norm_mix[i]), wmg=mg,
                  wbs=w_branch_ssm[i].astype(bf16), wbg=w_branch_gdn[i].astype(bf16), wout=w_out[i].astype(bf16),
                  nf=row(norm_ffn[i]), wfi=w_ffn_in[i].astype(bf16), wfo=w_ffn_out[i].astype(bf16),
                  npl=row(norm_pl[i]), wpg=w_pl_gate[i].astype(bf16), wpp=w_pl_proj[i].astype(bf16),
                  nfin=row(norm_final)),
    )


def _pick_tile(n, candidates):
    for c in candidates:
        if n % c == 0:
            return c
    raise ValueError(f"no tile for {n}")


def kernel(x_prompt, x_sample, p_prompt, p_sample, state_ssm, state_ssm_conv, state_gdn, state_gdn_conv,
           norm_mix, w_in, ssm_conv_w, ssm_conv_b, ssm_dt_bias, ssm_a_log, ssm_d, ssm_norm,
           gdn_conv_w, gdn_dt_bias, gdn_a_log, gdn_norm, w_branch_ssm, w_branch_gdn, w_out,
           norm_ffn, w_ffn_in, w_ffn_out, norm_pl, w_pl_gate, w_pl_proj, norm_final):
    depth = p_prompt.shape[0]
    bp, seq, d_model = x_prompt.shape
    bs, dec_seq, _ = x_sample.shape
    assert dec_seq == 1 and seq % STEP_TOKENS == 0 and bs % SAMPLE_BB == 0 and bs % SAMPLE_BB_GDN == 0
    h_m, h_g = ssm_dt_bias.shape[1], gdn_dt_bias.shape[1]
    d_inner = h_m * P_M
    tail_rows = slice(SUBLANES - (CONV_K - 1), SUBLANES)

    xp = x_prompt.reshape(bp * seq, d_model)
    xs = x_sample.reshape(bs, d_model)
    new_p = ([], [], [], [])
    new_s = ([], [], [], [])
    for i in range(depth):
        lw = _layer_weights(i, d_model, norm_mix, w_in, ssm_conv_w, ssm_conv_b, ssm_dt_bias, ssm_a_log, ssm_d,
                            ssm_norm, gdn_conv_w, gdn_dt_bias, gdn_a_log, gdn_norm, w_branch_ssm, w_branch_gdn,
                            w_out, norm_ffn, w_ffn_in, w_ffn_out, norm_pl, w_pl_gate, w_pl_proj, norm_final)
        last = i == depth - 1

        tp = bp * seq
        y, st_ssm, cs_ssm = _ssd_prompt(xp, lw["norm_mix"], lw["w_ssd"], bp, seq, **lw["ssm"])
        o, st_gdn, cs_gdn = _gdn_prompt(xp, lw["norm_mix"], lw["w_gdn"], bp, seq, n_heads=h_g, **lw["gdn"])
        xp = _tail(xp, y, o, p_prompt[i].reshape(tp, -1), lw["tail"], _pick_tile(tp, (256, 128)), last)
        new_p[0].append(st_ssm.reshape(bp, h_m, P_M, N_M))
        new_p[1].append(cs_ssm[:, tail_rows, :])
        new_p[2].append(st_gdn)
        new_p[3].append(cs_gdn[:, tail_rows, :])

        ys, cs_s, ss_s = _ssd_sample(_inproj(xs, lw["norm_mix"], lw["w_ssd"]),
                                     jnp.swapaxes(state_ssm_conv[i], 0, 1),
                                     state_ssm[i].reshape(bs, d_inner, N_M), **lw["ssm"])
        os_, cs_g, ss_g = _gdn_sample(_inproj(xs, lw["norm_mix"], lw["w_gdn"]),
                                      jnp.swapaxes(state_gdn_conv[i], 0, 1), state_gdn[i], **lw["gdn"])
        xs = _tail(xs, ys, os_, p_sample[i].reshape(bs, -1), lw["tail"],
                   _pick_tile(bs, (128, 64, 32, 16, 8)), last)
        new_s[0].append(ss_s.reshape(bs, h_m, P_M, N_M))
        new_s[1].append(jnp.swapaxes(cs_s, 0, 1))
        new_s[2].append(ss_g)
        new_s[3].append(jnp.swapaxes(cs_g, 0, 1))

    stack = lambda lst: jnp.stack(lst)
    return (xp.reshape(bp, seq, d_model), xs.reshape(bs, 1, d_model),
            stack(new_p[0]), stack(new_p[1]), stack(new_p[2]), stack(new_p[3]),
            stack(new_s[0]), stack(new_s[1]), stack(new_s[2]), stack(new_s[3]))
```

```python
import functools

import jax
import jax.numpy as jnp
from jax import lax
from jax.experimental import pallas as pl
from jax.experimental.pallas import tpu as pltpu

f32 = jnp.float32
bf16 = jnp.bfloat16

EPS = 1e-6
CONV_K = 4
LANES = 128
SUBLANES = 8
MXU_DIM = 256
VMEM_LIMIT = 60 * 1024 * 1024

P_M = 64
N_M = 128
G_M = 4
DK = 128
DV = 128
SSD_BLOCK = 128
GDN_CHUNK = 64
GDN_BLOCK = 128
GDN_BASE = 16
STEP_TOKENS = 512
CONV_COLS = 512
SAMPLE_BB = 8
SAMPLE_BB_GDN = 16


def _nt(a, b):
    return lax.dot_general(a, b, (((1,), (1,)), ((), ())), preferred_element_type=f32)


def _tn(a, b):
    return lax.dot_general(a, b, (((0,), (0,)), ((), ())), preferred_element_type=f32)


def _mm(a, b):
    return jnp.dot(a, b, preferred_element_type=f32)


def _split3(x):
    hi = x.astype(bf16)
    r1 = x - hi.astype(f32)
    mid = r1.astype(bf16)
    lo = (r1 - mid.astype(f32)).astype(bf16)
    return hi, mid, lo


def _mm_sel_rhs(x, sel3):
    return _mm(jnp.concatenate(_split3(x), axis=1), sel3)


def _mm_sel_lhs(sel, x):
    return _mm(jnp.concatenate([sel] * 3, axis=1), jnp.concatenate(_split3(x), axis=0))


def _pack3_lanes(x, width):
    lane = lax.broadcasted_iota(jnp.int32, x.shape, 1)
    hi, mid, lo = (t.astype(f32) for t in _split3(jnp.where(lane < width, x, 0.0)))
    return (hi + pltpu.roll(mid, width, 1) + pltpu.roll(lo, 2 * width, 1)).astype(bf16)


def _silu(x):
    h = 0.5 * x
    return h + h * jnp.tanh(h)


def _softplus(x):
    return jnp.maximum(x, 0.0) + jnp.log1p(jnp.exp(-jnp.abs(x)))


def _rms(x, g):
    return x * lax.rsqrt(jnp.mean(x * x, axis=-1, keepdims=True) + EPS) * g


def _iota2(shape):
    return lax.broadcasted_iota(jnp.int32, shape, 0), lax.broadcasted_iota(jnp.int32, shape, 1)


def _inproj_kernel(x_ref, g_ref, *refs):
    n = len(refs) // 2
    h = _rms(x_ref[...], g_ref[...]).astype(bf16)
    for w_ref, o_ref in zip(refs[:n], refs[n:]):
        o_ref[...] = _nt(h, w_ref[...])


def _inproj(x, g, ws):
    t, d = x.shape
    full = lambda shape: pl.BlockSpec(shape, lambda i: (0, 0))
    return pl.pallas_call(
        _inproj_kernel,
        grid=(1,),
        in_specs=[full((t, d)), full((1, d))] + [full(w.shape) for w in ws],
        out_specs=[full((t, w.shape[0])) for w in ws],
        out_shape=[jax.ShapeDtypeStruct((t, w.shape[0]), f32) for w in ws],
        compiler_params=pltpu.CompilerParams(dimension_semantics=("arbitrary",), vmem_limit_bytes=VMEM_LIMIT),
        name="inproj",
    )(x, g, *ws)


def _project_conv(h, w_ref, ext_ref, act_ref, cw_ref, cb_ref, rows, width):
    for c0 in range(0, width, CONV_COLS):
        sl = slice(c0, c0 + CONV_COLS)
        u = _nt(h, w_ref[sl, :])
        ext_ref[SUBLANES:SUBLANES + rows, sl] = u
        cw = cw_ref[:, sl]
        conv = u * cw[3:4]
        full = ext_ref[:, sl]
        for j in range(CONV_K - 1):
            shifted = pltpu.roll(full, CONV_K - 1 - j, 0)
            conv = conv + shifted[SUBLANES:SUBLANES + rows] * cw[j:j + 1]
        if cb_ref is not None:
            conv = conv + cb_ref[:, sl]
        act_ref[:, sl] = _silu(conv)
        ext_ref[0:SUBLANES, sl] = ext_ref[rows:rows + SUBLANES, sl]


def _ssd_kernel(x_ref, g_ref, wx_ref, wz_ref, wdt_ref, cw_ref, cb_ref, dtb_ref, aneg_ref, dexp_ref, nw_ref, e_ref,
                y_ref, st_out_ref, cs_out_ref, ext_ref, act_ref, z_ref, st_ref, yd_ref):
    i = pl.program_id(1)
    rows = x_ref.shape[0]
    c = SSD_BLOCK
    conv_w = act_ref.shape[1]
    d_inner = y_ref.shape[1]
    gw = d_inner // G_M
    hg = gw // P_M
    n_heads = d_inner // P_M

    @pl.when(i == 0)
    def _():
        ext_ref[0:SUBLANES, :] = jnp.zeros((SUBLANES, conv_w), f32)
        st_ref[...] = jnp.zeros(st_ref.shape, f32)

    h = _rms(x_ref[...], g_ref[...]).astype(bf16)
    _project_conv(h, wx_ref, ext_ref, act_ref, cw_ref, cb_ref, rows, conv_w)
    dt_raw = _nt(h, wdt_ref[...])
    for g in range(G_M):
        z_ref[:, g * gw:(g + 1) * gw] = _nt(h, wz_ref[g * gw:(g + 1) * gw, :])

    r, cc = _iota2((c, c))
    lower = r >= cc
    lower_b = jnp.where(lower, 1.0, 0.0).astype(bf16)
    upper3 = jnp.concatenate([jnp.where(r <= cc, 1.0, 0.0).astype(bf16)] * 3, axis=0)

    for sub in range(rows // c):
        rs = slice(sub * c, (sub + 1) * c)
        dt = _softplus(dt_raw[rs] + dtb_ref[...])
        da = dt * aneg_ref[...]
        acum = _mm_sel_lhs(lower_b, da)
        acum_t = _mm_sel_rhs(da.T, upper3)
        alast = acum[c - 1:c, :]
        dout = jnp.exp(alast - acum)
        scal = _pack3_lanes(jnp.concatenate(
            [dt, dt * dout, jnp.exp(acum), jnp.broadcast_to(jnp.exp(alast), (SUBLANES, LANES))], axis=0), n_heads)
        for g in range(G_M):
            gs = slice(g * gw, (g + 1) * gw)
            se = _mm(scal, e_ref[:, gs])
            dt_e, dd_e, ea_e, cd_e = se[0:c], se[c:2 * c], se[2 * c:3 * c], se[3 * c:3 * c + 1]
            xs = act_ref[rs, gs]
            bm = act_ref[rs, d_inner + g * N_M:d_inner + (g + 1) * N_M].astype(bf16)
            cm = act_ref[rs, d_inner + (G_M + g) * N_M:d_inner + (G_M + g + 1) * N_M].astype(bf16)
            xdt = (xs * dt_e).astype(bf16)
            cb = _nt(cm, bm).astype(bf16)
            st = st_ref[:, gs]
            y_off = _mm(cm, st.astype(bf16)) * ea_e
            st_ref[:, gs] = st * cd_e + _tn(bm, (xs * dd_e).astype(bf16))
            for hh in range(hg):
                hd = g * hg + hh
                lmat = jnp.exp(jnp.where(lower, acum[:, hd:hd + 1] - acum_t[hd:hd + 1, :], -jnp.inf))
                yd_ref[:, hh * P_M:(hh + 1) * P_M] = _mm(cb * lmat.astype(bf16), xdt[:, hh * P_M:(hh + 1) * P_M])
            y = yd_ref[...] + y_off + xs * dexp_ref[:, gs]
            y = y * _silu(z_ref[rs, gs])
            y_ref[rs, gs] = _rms(y, nw_ref[:, gs]).astype(y_ref.dtype)

    @pl.when(i == pl.num_programs(1) - 1)
    def _():
        cs_out_ref[0] = ext_ref[0:SUBLANES, :]
        for j in range(d_inner // LANES):
            st_out_ref[0, j * LANES:(j + 1) * LANES, :] = st_ref[:, j * LANES:(j + 1) * LANES].T


def _ssd_prompt(x, g, ws, bsz, seq, cw, cb, dtb, aneg, dexp, nw, e_mat):
    d_model = x.shape[1]
    d_inner = dexp.shape[1]
    conv_w = cw.shape[1]
    rows = STEP_TOKENS
    nb = seq // rows
    tok = lambda b, i: (b * nb + i, 0)
    res = lambda a: pl.BlockSpec(a.shape, lambda b, i: (0, 0), pipeline_mode=pl.Buffered(1))
    return pl.pallas_call(
        _ssd_kernel,
        grid=(bsz, nb),
        in_specs=[pl.BlockSpec((rows, d_model), tok)] + [res(a) for a in (g, *ws, cw, cb, dtb, aneg, dexp, nw, e_mat)],
        out_specs=[
            pl.BlockSpec((rows, d_inner), tok),
            pl.BlockSpec((1, d_inner, N_M), lambda b, i: (b, 0, 0)),
            pl.BlockSpec((1, SUBLANES, conv_w), lambda b, i: (b, 0, 0)),
        ],
        out_shape=[
            jax.ShapeDtypeStruct((bsz * seq, d_inner), bf16),
            jax.ShapeDtypeStruct((bsz, d_inner, N_M), f32),
            jax.ShapeDtypeStruct((bsz, SUBLANES, conv_w), f32),
        ],
        scratch_shapes=[
            pltpu.VMEM((rows + SUBLANES, conv_w), f32),
            pltpu.VMEM((rows, conv_w), f32),
            pltpu.VMEM((rows, d_inner), f32),
            pltpu.VMEM((N_M, d_inner), f32),
            pltpu.VMEM((SSD_BLOCK, d_inner // G_M), f32),
        ],
        compiler_params=pltpu.CompilerParams(
            dimension_semantics=("parallel", "arbitrary"), vmem_limit_bytes=VMEM_LIMIT),
        name="ssd_prompt",
    )(x, g, *ws, cw, cb, dtb, aneg, dexp, nw, e_mat)


def _gdn_kernel(x_ref, g_ref, wx_ref, wgate_ref, wba_ref, cw_ref, dtb_ref, aneg_ref, nw_ref,
                o_ref, st_out_ref, cs_out_ref, ext_ref, act_ref, gate_ref, st_ref):
    i = pl.program_id(1)
    rows = x_ref.shape[0]
    blk = GDN_BLOCK
    ck = GDN_CHUNK
    conv_w = act_ref.shape[1]
    n_heads = st_ref.shape[0]
    hk = n_heads * DK
    per_blk = blk // ck
    heads = range(n_heads)

    @pl.when(i == 0)
    def _():
        ext_ref[0:SUBLANES, :] = jnp.zeros((SUBLANES, conv_w), f32)
        st_ref[...] = jnp.zeros(st_ref.shape, f32)

    h_in = _rms(x_ref[...], g_ref[...]).astype(bf16)
    _project_conv(h_in, wx_ref, ext_ref, act_ref, cw_ref, None, rows, conv_w)
    ba = _nt(h_in, wba_ref[...])
    gate_ref[...] = _nt(h_in, wgate_ref[...])

    r, cc = _iota2((blk, blk))
    same = (r // ck) == (cc // ck)
    lower = same & (r >= cc)
    strict = same & (r > cc)
    lower_b = jnp.where(lower, 1.0, 0.0).astype(bf16)
    upper3 = jnp.concatenate([jnp.where(same & (r <= cc), 1.0, 0.0).astype(bf16)] * 3, axis=0)
    rcol = r[:, 0:1]
    eye = jnp.where(r == cc, 1.0, 0.0)
    blk_masks = []
    size = GDN_BASE
    while size <= ck:
        blk_masks.append((r // size) == (cc // size))
        size *= 2

    n_sub = rows // blk
    prep = [None] * n_sub
    xs = [None] * n_sub
    qks, qgs, kgs, gls = ([None] * (n_sub * n_heads) for _ in range(4))

    def prepare(sub):
        rs = slice(sub * blk, (sub + 1) * blk)
        beta = jax.nn.sigmoid(ba[rs, :LANES])
        glog = aneg_ref[...] * _softplus(ba[rs, LANES:] + dtb_ref[...])
        gcum = _mm_sel_lhs(lower_b, glog)
        gcum_t = _mm_sel_rhs(glog.T, upper3)
        nmats, rhs = [], []
        for h in heads:
            q = act_ref[rs, h * DK:(h + 1) * DK]
            k = act_ref[rs, hk + h * DK:hk + (h + 1) * DK]
            v = act_ref[rs, 2 * hk + h * DV:2 * hk + (h + 1) * DV]
            qn = q * lax.rsqrt(jnp.sum(q * q, axis=-1, keepdims=True) + EPS) * (DK ** -0.5)
            kn = k * lax.rsqrt(jnp.sum(k * k, axis=-1, keepdims=True) + EPS)
            bcol = beta[:, h:h + 1]
            gcol = gcum[:, h:h + 1]
            eg = jnp.exp(gcol)
            dm = jnp.exp(jnp.where(lower, gcol - gcum_t[h:h + 1, :], -jnp.inf))
            knb = kn.astype(bf16)
            kb = kn * bcol
            nmats.append(jnp.where(strict, -(_nt(kb.astype(bf16), knb) * dm), 0.0))
            rhs.append(jnp.concatenate([v * bcol, kb * eg], axis=-1))
            glast = [gcol[(j + 1) * ck - 1:(j + 1) * ck] for j in range(per_blk)]
            glast_row = glast[per_blk - 1]
            for j in range(per_blk - 2, -1, -1):
                glast_row = jnp.where(rcol < (j + 1) * ck, glast[j], glast_row)
            idx = sub * n_heads + h
            qks[idx] = jnp.where(lower, _nt(qn.astype(bf16), knb) * dm, 0.0).astype(bf16)
            qgs[idx] = (qn * eg).astype(bf16)
            kgs[idx] = (kn * jnp.exp(glast_row - gcol)).astype(bf16)
            gls[idx] = [jnp.exp(gl) for gl in glast]
            yield
        prep[sub] = (nmats, rhs)

    def solve(sub):
        nmats, rhs = prep[sub]
        qs = [jnp.where(blk_masks[0], n, 0.0) for n in nmats]
        minv = [eye + q for q in qs]
        span = 2
        while span < GDN_BASE:
            qbs = [q.astype(bf16) for q in qs]
            qs = [_mm(qb, qb) for qb in qbs]
            yield
            minv = [m + _mm(m.astype(bf16), q.astype(bf16)) for m, q in zip(minv, qs)]
            yield
            span *= 2
        for lvl in range(1, len(blk_masks)):
            off = blk_masks[lvl] & jnp.logical_not(blk_masks[lvl - 1])
            mbs = [m.astype(bf16) for m in minv]
            ems = [_mm(jnp.where(off, n, 0.0).astype(bf16), mb) for n, mb in zip(nmats, mbs)]
            yield
            minv = [m + _mm(mb, em.astype(bf16)) for m, mb, em in zip(minv, mbs, ems)]
            yield
        xs[sub] = [_mm(m.astype(bf16), x.astype(bf16)) for m, x in zip(minv, rhs)]

    state = {"s": [st_ref[h] for h in heads]}
    zeros = jnp.zeros((ck, DV), bf16)

    def recur(sub):
        for j in range(per_blk):
            rl = slice(j * ck, (j + 1) * ck)
            ro = slice(sub * blk + j * ck, sub * blk + (j + 1) * ck)
            at = lambda lst, h: lst[sub * n_heads + h]
            ss = state["s"]
            sbs = [s.astype(bf16) for s in ss]
            vns = [xs[sub][h][rl, :DV] - _mm(xs[sub][h][rl, DV:].astype(bf16), sbs[h]) for h in heads]
            yield
            vfull = [jnp.concatenate([zeros] * j + [vns[h].astype(bf16)] + [zeros] * (per_blk - 1 - j), axis=0)
                     for h in heads]
            outs = [_mm(at(qgs, h)[rl], sbs[h]) + _mm(at(qks, h)[rl, :], vfull[h]) for h in heads]
            yield
            state["s"] = [ss[h] * at(gls, h)[j] + _tn(at(kgs, h)[rl], vns[h].astype(bf16)) for h in heads]
            yield
            for h in heads:
                gt = gate_ref[ro, h * DV:(h + 1) * DV]
                o_ref[ro, h * DV:(h + 1) * DV] = (_rms(outs[h], nw_ref[...]) * _silu(gt)).astype(o_ref.dtype)
            yield

    def emit(*gens):
        live = list(gens)
        while live:
            for gen in list(live):
                if next(gen, StopIteration) is StopIteration:
                    live.remove(gen)

    emit(prepare(0))
    for sub in range(n_sub):
        later = [prepare(sub + 1)] if sub + 1 < n_sub else []
        earlier = [recur(sub - 1)] if sub > 0 else []
        emit(solve(sub), *later, *earlier)
    emit(recur(n_sub - 1))
    for h in heads:
        st_ref[h] = state["s"][h]

    @pl.when(i == pl.num_programs(1) - 1)
    def _():
        cs_out_ref[0] = ext_ref[0:SUBLANES, :]
        st_out_ref[0] = st_ref[...]


def _gdn_prompt(x, g, ws, bsz, seq, cw, dtb, aneg, nw, n_heads):
    d_model = x.shape[1]
    conv_w = cw.shape[1]
    hv = n_heads * DV
    rows = STEP_TOKENS
    nb = seq // rows
    tok = lambda b, i: (b * nb + i, 0)
    res = lambda a: pl.BlockSpec(a.shape, lambda b, i: (0, 0), pipeline_mode=pl.Buffered(1))
    return pl.pallas_call(
        _gdn_kernel,
        grid=(bsz, nb),
        in_specs=[pl.BlockSpec((rows, d_model), tok)] + [res(a) for a in (g, *ws, cw, dtb, aneg, nw)],
        out_specs=[
            pl.BlockSpec((rows, hv), tok),
            pl.BlockSpec((1, n_heads, DK, DV), lambda b, i: (b, 0, 0, 0)),
            pl.BlockSpec((1, SUBLANES, conv_w), lambda b, i: (b, 0, 0)),
        ],
        out_shape=[
            jax.ShapeDtypeStruct((bsz * seq, hv), bf16),
            jax.ShapeDtypeStruct((bsz, n_heads, DK, DV), f32),
            jax.ShapeDtypeStruct((bsz, SUBLANES, conv_w), f32),
        ],
        scratch_shapes=[
            pltpu.VMEM((rows + SUBLANES, conv_w), f32),
            pltpu.VMEM((rows, conv_w), f32),
            pltpu.VMEM((rows, hv), f32),
            pltpu.VMEM((n_heads, DK, DV), f32),
        ],
        compiler_params=pltpu.CompilerParams(
            dimension_semantics=("parallel", "arbitrary"), vmem_limit_bytes=VMEM_LIMIT),
        name="gdn_prompt",
    )(x, g, *ws, cw, dtb, aneg, nw)


def _conv_step(u, cs_ref, cs_out_ref, cw_ref, cb_ref):
    cw = cw_ref[...]
    conv = u * cw[3:4]
    for j in range(CONV_K - 1):
        conv = conv + cs_ref[j] * cw[j:j + 1]
    if cb_ref is not None:
        conv = conv + cb_ref[...]
    for j in range(CONV_K - 2):
        cs_out_ref[j] = cs_ref[j + 1]
    cs_out_ref[CONV_K - 2] = u
    return _silu(conv)


def _ssd_step_kernel(xbc_ref, z_ref, dt_ref, cs_ref, st_ref, cw_ref, cb_ref, dtb_ref, aneg_ref, dexp_ref,
                     nw_ref, e_ref, y_ref, cs_out_ref, st_out_ref, yacc_ref):
    bb = xbc_ref.shape[0]
    d_inner = y_ref.shape[1]
    gw = d_inner // G_M
    act = _conv_step(xbc_ref[...], cs_ref, cs_out_ref, cw_ref, cb_ref)
    z = z_ref[...]
    dt = _softplus(dt_ref[...] + dtb_ref[...])
    dec = jnp.exp(dt * aneg_ref[...])
    se = _mm(_pack3_lanes(jnp.concatenate([dt, dec], axis=0), d_inner // P_M), e_ref[...])
    xs = act[:, :d_inner]
    xdt = xs * se[0:bb]
    dec_e = se[bb:2 * bb]
    d3 = jnp.concatenate(_split3(dec_e), axis=0)
    r3, c3 = _iota2((3 * bb, bb * N_M))
    seq3 = c3 // N_M
    onehot3 = jnp.where((r3 == seq3) | (r3 == seq3 + bb) | (r3 == seq3 + 2 * bb), 1.0, 0.0).astype(bf16)
    rb, cb_i = _iota2((bb, bb * N_M))
    own = rb == cb_i // N_M
    rowid = rb[:, 0:1]

    for g in range(G_M):
        gs = slice(g * gw, (g + 1) * gw)
        bm = act[:, d_inner + g * N_M:d_inner + (g + 1) * N_M]
        cm = act[:, d_inner + (G_M + g) * N_M:d_inner + (G_M + g + 1) * N_M].astype(bf16)
        bdiag = jnp.where(own, jnp.concatenate([bm] * bb, axis=1), 0.0).astype(bf16)
        dcol = _tn(d3[:, gs], onehot3)
        outer = _tn(xdt[:, gs].astype(bf16), bdiag)
        yg = None
        for b in range(bb):
            bs = slice(b * N_M, (b + 1) * N_M)
            s_new = st_ref[b, gs, :] * dcol[:, bs] + outer[:, bs]
            st_out_ref[b, gs, :] = s_new
            yb = _nt(cm, s_new.astype(bf16))
            yg = yb if yg is None else jnp.where(rowid == b, yb, yg)
        yacc_ref[:, gs] = yg

    y = yacc_ref[...] + xs * dexp_ref[...]
    y = y * _silu(z)
    for g in range(G_M):
        gs = slice(g * gw, (g + 1) * gw)
        y_ref[:, gs] = _rms(y[:, gs], nw_ref[:, gs]).astype(y_ref.dtype)


def _ssd_sample(projs, conv_state, state, cw, cb, dtb, aneg, dexp, nw, e_mat):
    bsz = projs[0].shape[0]
    d_inner = dexp.shape[1]
    conv_w = cw.shape[1]
    bb = SAMPLE_BB
    const = lambda a: pl.BlockSpec(a.shape, lambda i: (0, 0))
    return pl.pallas_call(
        _ssd_step_kernel,
        grid=(bsz // bb,),
        in_specs=[pl.BlockSpec((bb, p.shape[1]), lambda i: (i, 0)) for p in projs] + [
            pl.BlockSpec((CONV_K - 1, bb, conv_w), lambda i: (0, i, 0)),
            pl.BlockSpec((bb, d_inner, N_M), lambda i: (i, 0, 0)),
        ] + [const(a) for a in (cw, cb, dtb, aneg, dexp, nw, e_mat)],
        out_specs=[
            pl.BlockSpec((bb, d_inner), lambda i: (i, 0)),
            pl.BlockSpec((CONV_K - 1, bb, conv_w), lambda i: (0, i, 0)),
            pl.BlockSpec((bb, d_inner, N_M), lambda i: (i, 0, 0)),
        ],
        out_shape=[
            jax.ShapeDtypeStruct((bsz, d_inner), bf16),
            jax.ShapeDtypeStruct((CONV_K - 1, bsz, conv_w), f32),
            jax.ShapeDtypeStruct((bsz, d_inner, N_M), f32),
        ],
        scratch_shapes=[pltpu.VMEM((bb, d_inner), f32)],
        compiler_params=pltpu.CompilerParams(
            dimension_semantics=("parallel",), vmem_limit_bytes=VMEM_LIMIT),
        name="ssd_sample",
    )(*projs, conv_state, state, cw, cb, dtb, aneg, dexp, nw, e_mat)


def _gdn_step_kernel(qkv_ref, gate_ref, ba_ref, cs_ref, st_ref, cw_ref, dtb_ref, aneg_ref, nw_ref,
                     o_ref, cs_out_ref, st_out_ref):
    bb = qkv_ref.shape[0]
    n_heads = st_ref.shape[1]
    hk = n_heads * DK
    act = _conv_step(qkv_ref[...], cs_ref, cs_out_ref, cw_ref, None)
    beta = jax.nn.sigmoid(ba_ref[:, :LANES])
    eg = jnp.exp(aneg_ref[...] * _softplus(ba_ref[:, LANES:] + dtb_ref[...]))
    rb, cb_i = _iota2((bb, bb * DV))
    own = rb == cb_i // DV
    rowid = rb[:, 0:1]

    for h in range(n_heads):
        q = act[:, h * DK:(h + 1) * DK]
        k = act[:, hk + h * DK:hk + (h + 1) * DK]
        v = act[:, 2 * hk + h * DV:2 * hk + (h + 1) * DV]
        qn = q * lax.rsqrt(jnp.sum(q * q, axis=-1, keepdims=True) + EPS) * (DK ** -0.5)
        kn = k * lax.rsqrt(jnp.sum(k * k, axis=-1, keepdims=True) + EPS)
        bcol = beta[:, h:h + 1]
        ecol = eg[:, h:h + 1]
        kb = kn * bcol
        u = v * bcol
        qk = jnp.sum(qn.astype(bf16).astype(f32) * kn.astype(bf16).astype(f32), axis=-1, keepdims=True)
        wq = jnp.concatenate([(kb * ecol).astype(bf16), (qn * ecol).astype(bf16)], axis=0)
        vn = o = None
        for b in range(bb):
            ws = _mm(wq, st_ref[b, h].astype(bf16))
            vn_b = u - ws[:bb]
            o_b = ws[bb:] + qk * vn_b
            vn = vn_b if vn is None else jnp.where(rowid == b, vn_b, vn)
            o = o_b if o is None else jnp.where(rowid == b, o_b, o)
        vdiag = jnp.where(own, jnp.concatenate([vn] * bb, axis=1), 0.0).astype(bf16)
        outer = _tn(kn.astype(bf16), vdiag)
        for b in range(bb):
            st_out_ref[b, h] = st_ref[b, h] * ecol[b:b + 1, :] + outer[:, b * DV:(b + 1) * DV]
        gt = gate_ref[:, h * DV:(h + 1) * DV]
        o_ref[:, h * DV:(h + 1) * DV] = (_rms(o, nw_ref[...]) * _silu(gt)).astype(o_ref.dtype)


def _gdn_sample(projs, conv_state, state, cw, dtb, aneg, nw):
    bsz = projs[0].shape[0]
    n_heads = state.shape[1]
    conv_w = cw.shape[1]
    hv = n_heads * DV
    bb = SAMPLE_BB_GDN
    const = lambda a: pl.BlockSpec(a.shape, lambda i: (0, 0))
    return pl.pallas_call(
        _gdn_step_kernel,
        grid=(bsz // bb,),
        in_specs=[pl.BlockSpec((bb, p.shape[1]), lambda i: (i, 0)) for p in projs] + [
            pl.BlockSpec((CONV_K - 1, bb, conv_w), lambda i: (0, i, 0)),
            pl.BlockSpec((bb, n_heads, DK, DV), lambda i: (i, 0, 0, 0)),
        ] + [const(a) for a in (cw, dtb, aneg, nw)],
        out_specs=[
            pl.BlockSpec((bb, hv), lambda i: (i, 0)),
            pl.BlockSpec((CONV_K - 1, bb, conv_w), lambda i: (0, i, 0)),
            pl.BlockSpec((bb, n_heads, DK, DV), lambda i: (i, 0, 0, 0)),
        ],
        out_shape=[
            jax.ShapeDtypeStruct((bsz, hv), bf16),
            jax.ShapeDtypeStruct((CONV_K - 1, bsz, conv_w), f32),
            jax.ShapeDtypeStruct((bsz, n_heads, DK, DV), f32),
        ],
        compiler_params=pltpu.CompilerParams(
            dimension_semantics=("parallel",), vmem_limit_bytes=VMEM_LIMIT),
        name="gdn_sample",
    )(*projs, conv_state, state, cw, dtb, aneg, nw)


def _tail_kernel(x_ref, y_ref, o_ref, p_ref, nmix_ref, wmg_ref, wbs_ref, wbg_ref, wout_ref, nf_ref, wfi_ref,
                 wfo_ref, npl_ref, wpg_ref, wpp_ref, nfin_ref, out_ref, *, ff_chunks, final_norm):
    d = x_ref.shape[1]
    d_ff = wfo_ref.shape[0]
    x = x_ref[...]
    ys = _mm(y_ref[...], wbs_ref[...])
    og = _mm(o_ref[...], wbg_ref[...])
    mg = _nt(_rms(x, nmix_ref[...]).astype(bf16), wmg_ref[...])
    mix = jax.nn.sigmoid(mg[:, :d]) * ys + jax.nn.sigmoid(mg[:, d:]) * og
    x = x + _mm(mix.astype(bf16), wout_ref[...])
    pe = _mm(p_ref[...].astype(bf16), wpp_ref[...])
    h = _rms(x, nf_ref[...]).astype(bf16)
    tiles = d_ff // MXU_DIM
    bounds = [MXU_DIM * ((tiles * c + ff_chunks - 1) // ff_chunks) for c in range(ff_chunks)] + [d_ff]
    for lo, hi in zip(bounds[:-1], bounds[1:]):
        gt = _mm(h, wfi_ref[:, lo:hi])
        up = _mm(h, wfi_ref[:, d_ff + lo:d_ff + hi])
        x = x + _mm((_silu(gt) * up).astype(bf16), wfo_ref[lo:hi, :])
    x = x + pe * jax.nn.sigmoid(_mm(_rms(x, npl_ref[...]).astype(bf16), wpg_ref[...]))
    if final_norm:
        x = _rms(x, nfin_ref[...])
    out_ref[...] = x


def _tail(x, y, o, p, wts, tm, final_norm):
    t, d = x.shape
    tok = lambda w: pl.BlockSpec((tm, w), lambda i: (i, 0))
    res = lambda a: pl.BlockSpec(a.shape, lambda i: (0, 0), pipeline_mode=pl.Buffered(1))
    names = ("nmix", "wmg", "wbs", "wbg", "wout", "nf", "wfi", "wfo", "npl", "wpg", "wpp", "nfin")
    return pl.pallas_call(
        functools.partial(_tail_kernel, ff_chunks=2, final_norm=final_norm),
        grid=(t // tm,),
        in_specs=[tok(d), tok(y.shape[1]), tok(o.shape[1]), tok(p.shape[1])] + [res(wts[n]) for n in names],
        out_specs=tok(d),
        out_shape=jax.ShapeDtypeStruct((t, d), f32),
        compiler_params=pltpu.CompilerParams(
            dimension_semantics=("parallel",), vmem_limit_bytes=VMEM_LIMIT),
        name="tail",
    )(x, y, o, p, *[wts[n] for n in names])


def _pad_lanes(v, width=LANES):
    return jnp.pad(v, ((0, 0), (0, width - v.shape[1])))


def _pad_rows(v, height=LANES):
    return jnp.pad(v, ((0, height - v.shape[0]), (0, 0)))


def _layer_weights(i, d_model, norm_mix, w_in, ssm_conv_w, ssm_conv_b, ssm_dt_bias, ssm_a_log, ssm_d, ssm_norm,
                   gdn_conv_w, gdn_dt_bias, gdn_a_log, gdn_norm, w_branch_ssm, w_branch_gdn, w_out,
                   norm_ffn, w_ffn_in, w_ffn_out, norm_pl, w_pl_gate, w_pl_proj, norm_final):
    h_m = ssm_dt_bias.shape[1]
    h_g = gdn_dt_bias.shape[1]
    d_inner = h_m * P_M
    conv_m = ssm_conv_w.shape[2]
    conv_g = gdn_conv_w.shape[2]
    sizes = (d_inner, conv_m, h_m, conv_g, h_g * DV, h_g, h_g, 2 * d_model)
    starts = [0]
    for s in sizes:
        starts.append(starts[-1] + s)
    w_t = jnp.swapaxes(w_in[i], 0, 1)
    seg = lambda j: w_t[starts[j]:starts[j + 1]].astype(bf16)
    z, xbc, dtw, qkv, gate, bw, aw, mg = (seg(j) for j in range(8))
    w_ssd = (xbc, z, _pad_rows(dtw))
    w_gdn = (qkv, gate, jnp.concatenate([_pad_rows(bw), _pad_rows(aw)], axis=0))
    head_of = jnp.arange(d_inner, dtype=jnp.int32) // P_M
    assert 3 * h_m <= LANES
    rows_e = jnp.arange(LANES, dtype=jnp.int32)[:, None]
    e_mat = ((rows_e % h_m == head_of[None, :]) & (rows_e < 3 * h_m)).astype(bf16)
    row = lambda v: v.reshape(1, -1).astype(f32)
    return dict(
        norm_mix=row(norm_mix[i]), w_ssd=w_ssd, w_gdn=w_gdn,
        ssm=dict(cw=ssm_conv_w[i], cb=row(ssm_conv_b[i]), dtb=_pad_lanes(row(ssm_dt_bias[i])),
                 aneg=_pad_lanes(-jnp.exp(row(ssm_a_log[i]))), dexp=row(jnp.repeat(ssm_d[i], P_M)),
                 nw=row(ssm_norm[i]), e_mat=e_mat),
        gdn=dict(cw=gdn_conv_w[i], dtb=_pad_lanes(row(gdn_dt_bias[i])),
                 aneg=_pad_lanes(-jnp.exp(row(gdn_a_log[i]))), nw=row(gdn_norm[i])),
        tail=dict(nmix=row(norm_mix[i]), wmg=mg,
                  wbs=w_branch_ssm[i].astype(bf16), wbg=w_branch_gdn[i].astype(bf16), wout=w_out[i].astype(bf16),
                  nf=row(norm_ffn[i]), wfi=w_ffn_in[i].astype(bf16), wfo=w_ffn_out[i].astype(bf16),
                  npl=row(norm_pl[i]), wpg=w_pl_gate[i].astype(bf16), wpp=w_pl_proj[i].astype(bf16),
                  nfin=row(norm_final)),
    )


def _pick_tile(n, candidates):
    for c in candidates:
        if n % c == 0:
            return c
    raise ValueError(f"no tile for {n}")


def kernel(x_prompt, x_sample, p_prompt, p_sample, state_ssm, state_ssm_conv, state_gdn, state_gdn_conv,
           norm_mix, w_in, ssm_conv_w, ssm_conv_b, ssm_dt_bias, ssm_a_log, ssm_d, ssm_norm,
           gdn_conv_w, gdn_dt_bias, gdn_a_log, gdn_norm, w_branch_ssm, w_branch_gdn, w_out,
           norm_ffn, w_ffn_in, w_ffn_out, norm_pl, w_pl_gate, w_pl_proj, norm_final):
    depth = p_prompt.shape[0]
    bp, seq, d_model = x_prompt.shape
    bs, dec_seq, _ = x_sample.shape
    assert dec_seq == 1 and seq % STEP_TOKENS == 0 and bs % SAMPLE_BB == 0 and bs % SAMPLE_BB_GDN == 0
    h_m, h_g = ssm_dt_bias.shape[1], gdn_dt_bias.shape[1]
    d_inner = h_m * P_M
    tail_rows = slice(SUBLANES - (CONV_K - 1), SUBLANES)

    xp = x_prompt.reshape(bp * seq, d_model)
    xs = x_sample.reshape(bs, d_model)
    new_p = ([], [], [], [])
    new_s = ([], [], [], [])
    for i in range(depth):
        lw = _layer_weights(i, d_model, norm_mix, w_in, ssm_conv_w, ssm_conv_b, ssm_dt_bias, ssm_a_log, ssm_d,
                            ssm_norm, gdn_conv_w, gdn_dt_bias, gdn_a_log, gdn_norm, w_branch_ssm, w_branch_gdn,
                            w_out, norm_ffn, w_ffn_in, w_ffn_out, norm_pl, w_pl_gate, w_pl_proj, norm_final)
        last = i == depth - 1

        tp = bp * seq
        y, st_ssm, cs_ssm = _ssd_prompt(xp, lw["norm_mix"], lw["w_ssd"], bp, seq, **lw["ssm"])
        o, st_gdn, cs_gdn = _gdn_prompt(xp, lw["norm_mix"], lw["w_gdn"], bp, seq, n_heads=h_g, **lw["gdn"])
        xp = _tail(xp, y, o, p_prompt[i].reshape(tp, -1), lw["tail"], _pick_tile(tp, (512, 256, 128)), last)
        new_p[0].append(st_ssm.reshape(bp, h_m, P_M, N_M))
        new_p[1].append(cs_ssm[:, tail_rows, :])
        new_p[2].append(st_gdn)
        new_p[3].append(cs_gdn[:, tail_rows, :])

        ys, cs_s, ss_s = _ssd_sample(_inproj(xs, lw["norm_mix"], lw["w_ssd"]),
                                     jnp.swapaxes(state_ssm_conv[i], 0, 1),
                                     state_ssm[i].reshape(bs, d_inner, N_M), **lw["ssm"])
        os_, cs_g, ss_g = _gdn_sample(_inproj(xs, lw["norm_mix"], lw["w_gdn"]),
                                      jnp.swapaxes(state_gdn_conv[i], 0, 1), state_gdn[i], **lw["gdn"])
        xs = _tail(xs, ys, os_, p_sample[i].reshape(bs, -1), lw["tail"],
                   _pick_tile(bs, (128, 64, 32, 16, 8)), last)
        new_s[0].append(ss_s.reshape(bs, h_m, P_M, N_M))
        new_s[1].append(jnp.swapaxes(cs_s, 0, 1))
        new_s[2].append(ss_g)
        new_s[3].append(jnp.swapaxes(cs_g, 0, 1))

    stack = lambda lst: jnp.stack(lst)
    return (xp.reshape(bp, seq, d_model), xs.reshape(bs, 1, d_model),
            stack(new_p[0]), stack(new_p[1]), stack(new_p[2]), stack(new_p[3]),
            stack(new_s[0]), stack(new_s[1]), stack(new_s[2]), stack(new_s[3]))
```
